```python
import math
import jax, jax.numpy as jnp
from jax import lax
import numpy as np

D_MODEL = 1024
BATCH = 16
SEQ = 2048
DEPTH = 1

N_HEADS = 16
N_KV_HEADS = 4
HEAD_DIM = 64
ATTN_WIDTH = N_HEADS * HEAD_DIM
KV_WIDTH = N_KV_HEADS * HEAD_DIM
WINDOW = 128
BLOCK = 128
REL_BUCKETS = 32
REL_MAX_DIST = 128
NEG_INF = -1e30
HYENA_WIDTH = 1024
HYENA_ORDER = 2
SHORT_CONV = 3
FILTER_EMB = 33
FILTER_HIDDEN = 64
DECAY_FAST = 0.3
DECAY_SLOW = 1.5
DECAY_TARGET = 1e-2
N_BRANCHES = 2
SPLIT_SIZES = (ATTN_WIDTH, KV_WIDTH, KV_WIDTH, ATTN_WIDTH,
               (HYENA_ORDER + 1) * HYENA_WIDTH, HYENA_WIDTH, N_BRANCHES * D_MODEL)
IN_COLS = ATTN_WIDTH + 2 * KV_WIDTH + ATTN_WIDTH + (HYENA_ORDER + 1) * HYENA_WIDTH + HYENA_WIDTH + N_BRANCHES * D_MODEL
DEEPNORM_ALPHA = (2 * DEPTH) ** 0.25
DEEPNORM_BETA = (8 * DEPTH) ** -0.25
LN_EPS = 1e-5

kernel_name = "hybrid_swa_hyena_deepnorm_encoder"


def _t5_bucket(rel):
    half = REL_BUCKETS // 2
    max_exact = half // 2
    ret = (rel > 0).astype(np.int32) * half
    n = np.abs(rel)
    n_safe = np.maximum(n, 1).astype(np.float32)
    large = max_exact + (np.log(n_safe / max_exact) / math.log(REL_MAX_DIST / max_exact)
                         * (half - max_exact)).astype(np.int32)
    large = np.minimum(large, half - 1)
    return (ret + np.where(n < max_exact, n, large)).astype(np.int32)


def _layer_norm(x, g, b):
    xf = x.astype(jnp.float32)
    mu = xf.mean(-1, keepdims=True)
    var = jnp.square(xf - mu).mean(-1, keepdims=True)
    y = (xf - mu) * lax.rsqrt(var + LN_EPS) * g.astype(jnp.float32) + b.astype(jnp.float32)
    return y.astype(x.dtype)


def _windowed_gqa(q, k, v, rel_bias, sink):
    b, s = q.shape[0], q.shape[1]
    nb = s // BLOCK
    g = N_HEADS // N_KV_HEADS
    scale = HEAD_DIM ** -0.5
    q = q.reshape(b, s, N_KV_HEADS, g, HEAD_DIM)
    pad = ((0, 0), (BLOCK, BLOCK), (0, 0), (0, 0))
    kp = jnp.pad(k, pad)
    vp = jnp.pad(v, pad)
    a = np.arange(BLOCK)[:, None]
    c = np.arange(3 * BLOCK)[None, :]
    rel = c - BLOCK - a
    band = jnp.asarray(np.abs(rel) <= WINDOW)
    bias = rel_bias.astype(jnp.float32)[_t5_bucket(rel)]
    bias = bias.transpose(2, 0, 1).reshape(N_KV_HEADS, g, BLOCK, 3 * BLOCK)
    sink_l = sink.astype(jnp.float32).reshape(1, N_KV_HEADS, g, 1, 1)
    offs = jnp.arange(3 * BLOCK)

    def attend_block(n):
        start = n * BLOCK
        qb = lax.dynamic_slice_in_dim(q, start, BLOCK, axis=1)
        kb = lax.dynamic_slice_in_dim(kp, start, 3 * BLOCK, axis=1)
        vb = lax.dynamic_slice_in_dim(vp, start, 3 * BLOCK, axis=1)
        key_pos = start - BLOCK + offs
        valid = band & ((key_pos >= 0) & (key_pos < s))[None, :]
        sc = jnp.einsum('bqkgd,bckd->bkgqc', qb, kb,
                        preferred_element_type=jnp.float32) * scale + bias
        sc = jnp.where(valid, sc, NEG_INF)
        m = jnp.maximum(sc.max(-1, keepdims=True), sink_l)
        e = jnp.exp(sc - m)
        p = e / (e.sum(-1, keepdims=True) + jnp.exp(sink_l - m))
        return jnp.einsum('bkgqc,bckd->bqkgd', p.astype(vb.dtype), vb)

    out = lax.map(attend_block, jnp.arange(nb))
    return out.transpose(1, 0, 2, 3, 4, 5).reshape(b, s, ATTN_WIDTH)


def _short_conv(u, w, bias):
    s = u.shape[1]
    half = SHORT_CONV // 2
    up = jnp.pad(u, ((0, 0), (half, half), (0, 0)))
    y = bias
    for j in range(SHORT_CONV):
        y = y + up[:, j:j + s] * w[j]
    return y


def _hyena_filters(seq_len, w1, b1, w2, b2, w3, b3, w4, freq):
    f32 = jnp.float32
    bands = (FILTER_EMB - 1) // 2
    t = jnp.linspace(0.0, 1.0, seq_len, dtype=f32)[:, None]
    w = 2.0 * math.pi * jnp.arange(seq_len, dtype=f32)[:, None] / seq_len
    fb = jnp.linspace(1e-4, bands - 1, bands, dtype=f32)[None, :]
    z = jnp.concatenate([t, jnp.cos(fb * w), -jnp.sin(fb * w)], axis=-1)
    fr = freq.astype(f32)
    h = jnp.sin(fr * (z @ w1.astype(f32) + b1.astype(f32)))
    h = jnp.sin(fr * (h @ w2.astype(f32) + b2.astype(f32)))
    h = jnp.sin(fr * (h @ w3.astype(f32) + b3.astype(f32)))
    h = h @ w4.astype(f32)
    max_decay = math.log(DECAY_TARGET) / DECAY_FAST
    min_decay = math.log(DECAY_TARGET) / DECAY_SLOW
    deltas = jnp.linspace(min_decay, max_decay, HYENA_WIDTH, dtype=f32)
    decay = jnp.exp(-t * jnp.abs(deltas))
    h = h.reshape(seq_len, 2, HYENA_WIDTH) * decay[:, None, :]
    return h[:, 0], h[:, 1]


def _bidir_long_conv(u, h_fwd, h_bwd):
    L = u.shape[1]
    k = jnp.concatenate([h_fwd, jnp.zeros((1, h_fwd.shape[1]), h_fwd.dtype), h_bwd[:0:-1]], axis=0)
    K = jnp.fft.rfft(k, axis=0)
    U = jnp.fft.rfft(u.astype(jnp.float32), n=2 * L, axis=1)
    y = jnp.fft.irfft(U * K[None], n=2 * L, axis=1)[:, :L]
    return y.astype(u.dtype)


def _hybrid_layer(x, w_in, rel_bias, attn_sink, conv_w, conv_b,
                  filt_w1, filt_b1, filt_w2, filt_b2, filt_w3, filt_b3, filt_w4, filt_freq,
                  hyena_skip, w_branch_attn, w_branch_hyena, w_out, ln_g, ln_b):
    b, s, _ = x.shape
    u = x @ w_in
    idx = np.cumsum(SPLIT_SIZES)[:-1].tolist()
    q, k, v, a_gate, hy, h_gate, br_gate = jnp.split(u, idx, axis=-1)
    y_a = _windowed_gqa(q.reshape(b, s, N_HEADS, HEAD_DIM),
                        k.reshape(b, s, N_KV_HEADS, HEAD_DIM),
                        v.reshape(b, s, N_KV_HEADS, HEAD_DIM), rel_bias, attn_sink)
    y_a = y_a * jax.nn.silu(a_gate)
    hc = _short_conv(hy, conv_w, conv_b)
    x0, x1, hv = jnp.split(hc, HYENA_ORDER + 1, axis=-1)
    z = hv * x1
    h_f, h_b = _hyena_filters(s, filt_w1, filt_b1, filt_w2, filt_b2, filt_w3, filt_b3, filt_w4, filt_freq)
    z = _bidir_long_conv(z, h_f, h_b) + z * hyena_skip
    y_h = z * x0 * jax.nn.silu(h_gate)
    g_a, g_h = jnp.split(jax.nn.sigmoid(br_gate), N_BRANCHES, axis=-1)
    merged = g_a * (y_a @ w_branch_attn) + g_h * (y_h @ w_branch_hyena)
    out = merged @ w_out
    return _layer_norm(DEEPNORM_ALPHA * x + out, ln_g, ln_b)


def setup_inputs(seed: int = 0) -> dict:
    key = jax.random.key(seed)
    ks = jax.random.split(key, 24)
    f32 = jnp.float32
    nrm = lambda k, shape, sc: jax.random.normal(k, shape, f32) * sc
    L_ = DEPTH
    C3 = (HYENA_ORDER + 1) * HYENA_WIDTH
    return {
        "x": nrm(ks[0], (BATCH, SEQ, D_MODEL), 1.0),
        "w_in": nrm(ks[1], (L_, D_MODEL, IN_COLS), D_MODEL ** -0.5),
        "rel_bias": nrm(ks[2], (REL_BUCKETS, N_HEADS), 0.5),
        "attn_sink": nrm(ks[3], (L_, N_HEADS), 0.5),
        "conv_w": nrm(ks[4], (L_, SHORT_CONV, C3), SHORT_CONV ** -0.5),
        "conv_b": nrm(ks[5], (L_, C3), 0.01),
        "filt_w1": nrm(ks[6], (L_, FILTER_EMB, FILTER_HIDDEN), FILTER_EMB ** -0.5),
        "filt_b1": nrm(ks[7], (L_, FILTER_HIDDEN), 0.1),
        "filt_w2": nrm(ks[8], (L_, FILTER_HIDDEN, FILTER_HIDDEN), FILTER_HIDDEN ** -0.5),
        "filt_b2": nrm(ks[9], (L_, FILTER_HIDDEN), 0.1),
        "filt_w3": nrm(ks[10], (L_, FILTER_HIDDEN, FILTER_HIDDEN), FILTER_HIDDEN ** -0.5),
        "filt_b3": nrm(ks[11], (L_, FILTER_HIDDEN), 0.1),
        "filt_w4": nrm(ks[12], (L_, FILTER_HIDDEN, 2 * HYENA_WIDTH), 0.05 * FILTER_HIDDEN ** -0.5),
        "filt_freq": 1.0 + nrm(ks[13], (L_, FILTER_HIDDEN), 0.05),
        "hyena_skip": nrm(ks[14], (L_, HYENA_WIDTH), 1.0),
        "w_branch_attn": nrm(ks[15], (L_, ATTN_WIDTH, D_MODEL), DEEPNORM_BETA * ATTN_WIDTH ** -0.5),
        "w_branch_hyena": nrm(ks[16], (L_, HYENA_WIDTH, D_MODEL), DEEPNORM_BETA * HYENA_WIDTH ** -0.5),
        "w_out": nrm(ks[17], (L_, D_MODEL, D_MODEL), DEEPNORM_BETA * D_MODEL ** -0.5),
        "ln_g": 1.0 + nrm(ks[18], (L_, D_MODEL), 0.01),
        "ln_b": nrm(ks[19], (L_, D_MODEL), 0.01),
    }


def reference(x, w_in, rel_bias, attn_sink, conv_w, conv_b, filt_w1, filt_b1, filt_w2, filt_b2,
              filt_w3, filt_b3, filt_w4, filt_freq, hyena_skip, w_branch_attn, w_branch_hyena,
              w_out, ln_g, ln_b):
    h = x
    for l in range(DEPTH):
        h = _hybrid_layer(h, w_in[l], rel_bias, attn_sink[l], conv_w[l], conv_b[l],
                          filt_w1[l], filt_b1[l], filt_w2[l], filt_b2[l], filt_w3[l], filt_b3[l],
                          filt_w4[l], filt_freq[l], hyena_skip[l], w_branch_attn[l],
                          w_branch_hyena[l], w_out[l], ln_g[l], ln_b[l])
    return h
```

```python
import functools
import math

import numpy as np
import jax
import jax.numpy as jnp
from jax import lax
from jax.experimental import pallas as pl
from jax.experimental.pallas import tpu as pltpu

D_MODEL = 1024
SEQ = 2048
N_HEADS = 16
N_KV_HEADS = 4
HEAD_DIM = 64
GROUP = N_HEADS // N_KV_HEADS
ATTN_WIDTH = N_HEADS * HEAD_DIM
KV_WIDTH = N_KV_HEADS * HEAD_DIM
WINDOW = 128
BLOCK = 128
REL_BUCKETS = 32
REL_MAX_DIST = 128
NEG_INF = -1e30
HYENA_WIDTH = 1024
FILTER_EMB = 33
FILTER_HIDDEN = 64
DECAY_FAST = 0.3
DECAY_SLOW = 1.5
DECAY_TARGET = 1e-2
DEPTH = 1
DEEPNORM_ALPHA = (2 * DEPTH) ** 0.25
LN_EPS = 1e-5

COL_Q = 0
COL_K = ATTN_WIDTH
COL_V = COL_K + KV_WIDTH
COL_AGATE = COL_V + KV_WIDTH
COL_HY = COL_AGATE + ATTN_WIDTH
COL_HGATE = COL_HY + 3 * HYENA_WIDTH
COL_BR = COL_HGATE + HYENA_WIDTH
IN_COLS = COL_BR + 2 * D_MODEL

DFT_N = 2 * SEQ
RADIX = 4
SUB = SEQ // RADIX
CT = 256
LANE = 128
FEAT_PAD = 128

VMEM_LIMIT = 56 * 1024 * 1024

f32 = jnp.float32
bf16 = jnp.bfloat16


@functools.lru_cache(maxsize=None)
def _dft_matrices():
    f = np.arange(SUB, dtype=np.int64)[:, None]
    m = np.arange(SUB, dtype=np.int64)[None, :]
    w = np.empty((RADIX, 2 * SUB, SUB), np.float32)
    for r in range(RADIX):
        idx = ((2 * f + 1) * (RADIX * m + r)) % (2 * DFT_N)
        ang = np.pi * idx.astype(np.float64) / DFT_N
        w[r, :SUB] = np.cos(ang)
        w[r, SUB:] = -np.sin(ang)
    v = np.ascontiguousarray(np.transpose(w, (0, 2, 1))) * np.float32(2.0 / DFT_N)
    return w, v


@functools.lru_cache(maxsize=None)
def _filter_constants():
    bands = (FILTER_EMB - 1) // 2
    t = np.linspace(0.0, 1.0, SEQ, dtype=np.float32)[:, None]
    w = (2.0 * math.pi * np.arange(SEQ, dtype=np.float32)[:, None] / SEQ).astype(np.float32)
    fb = np.linspace(1e-4, bands - 1, bands, dtype=np.float32)[None, :]
    arg = (fb * w).astype(np.float64)
    feat = np.zeros((SEQ, FEAT_PAD), np.float32)
    feat[:, 0:1] = t
    feat[:, 1:1 + bands] = np.cos(arg)
    feat[:, 1 + bands:1 + 2 * bands] = -np.sin(arg)
    max_decay = math.log(DECAY_TARGET) / DECAY_FAST
    min_decay = math.log(DECAY_TARGET) / DECAY_SLOW
    deltas = np.linspace(min_decay, max_decay, HYENA_WIDTH, dtype=np.float32)
    decay = np.exp(-t.astype(np.float64) * np.abs(deltas.astype(np.float64))).astype(np.float32)
    return feat, decay


@functools.lru_cache(maxsize=None)
def _bucket_table():
    a = np.arange(BLOCK)[:, None]
    c = np.arange(3 * BLOCK)[None, :]
    rel = c - BLOCK - a
    half = REL_BUCKETS // 2
    max_exact = half // 2
    ret = (rel > 0).astype(np.int32) * half
    n = np.abs(rel)
    n_safe = np.maximum(n, 1).astype(np.float32)
    large = max_exact + (np.log(n_safe / max_exact) / math.log(REL_MAX_DIST / max_exact)
                         * (half - max_exact)).astype(np.int32)
    large = np.minimum(large, half - 1)
    bucket = (ret + np.where(n < max_exact, n, large)).astype(np.int32)
    return np.where(np.abs(rel) <= WINDOW, bucket, -1).astype(np.int32)


def _store_lane_split(ref, val):
    for j in range(ref.shape[0]):
        ref[j] = val[:, j * LANE:(j + 1) * LANE]


def _load_lane_split(ref, rows=slice(None)):
    return jnp.concatenate([ref[j, rows, :] for j in range(ref.shape[0])], axis=1)


def _dft_forward(z_ref, w_ref):
    a = []
    for r in range(RADIX):
        zr = _load_lane_split(z_ref, pl.ds(r, SUB, stride=RADIX)).astype(bf16)
        ar = jnp.dot(w_ref[r], zr, preferred_element_type=f32)
        a.append((ar[:SUB], ar[SUB:]))
    s02 = (a[0][0] + a[2][0], a[0][1] + a[2][1])
    d02 = (a[0][0] - a[2][0], a[0][1] - a[2][1])
    s13 = (a[1][0] + a[3][0], a[1][1] + a[3][1])
    d13 = (a[1][0] - a[3][0], a[1][1] - a[3][1])
    x0 = (s02[0] + s13[0], s02[1] + s13[1])
    x2 = (s02[0] - s13[0], s02[1] - s13[1])
    x1 = (d02[0] + d13[1], d02[1] - d13[0])
    x3 = (d02[0] - d13[1], d02[1] + d13[0])
    return [x0, x1, x2, x3]


def _dft_inverse(p, v_ref, y_ref):
    s02 = (p[0][0] + p[2][0], p[0][1] + p[2][1])
    d02 = (p[0][0] - p[2][0], p[0][1] - p[2][1])
    s13 = (p[1][0] + p[3][0], p[1][1] + p[3][1])
    d13 = (p[1][0] - p[3][0], p[1][1] - p[3][1])
    q = [
        (s02[0] + s13[0], s02[1] + s13[1]),
        (d02[0] - d13[1], d02[1] + d13[0]),
        (s02[0] - s13[0], s02[1] - s13[1]),
        (d02[0] + d13[1], d02[1] - d13[0]),
    ]
    for r in range(RADIX):
        qr = jnp.concatenate([q[r][0], q[r][1]], axis=0).astype(bf16)
        yr = jnp.dot(v_ref[r], qr, preferred_element_type=f32)
        for j in range(y_ref.shape[0]):
            y_ref[j, pl.ds(r, SUB, stride=RADIX), :] = yr[:, j * LANE:(j + 1) * LANE]


def _filter_kernel(feat_ref, w1_ref, b1_ref, w2_ref, b2_ref, w3_ref, b3_ref, fr_ref,
                   w4f_ref, w4b_ref, decay_ref, wdft_ref, kre_ref, kim_ref, h3_ref, sig_ref):
    hi = lax.Precision.HIGHEST

    @pl.when(pl.program_id(0) == 0)
    def _():
        fr = fr_ref[...]
        h = jnp.sin(fr * (jnp.dot(feat_ref[...], w1_ref[...], precision=hi,
                                  preferred_element_type=f32) + b1_ref[...]))
        h = jnp.sin(fr * (jnp.dot(h, w2_ref[...], precision=hi, preferred_element_type=f32) + b2_ref[...]))
        h = jnp.sin(fr * (jnp.dot(h, w3_ref[...], precision=hi, preferred_element_type=f32) + b3_ref[...]))
        h3_ref[...] = h

    h3 = h3_ref[...]
    decay = decay_ref[...]
    hf = jnp.dot(h3, w4f_ref[...], precision=hi, preferred_element_type=f32) * decay
    hb = jnp.dot(h3, w4b_ref[...], precision=hi, preferred_element_type=f32) * decay
    row = lax.broadcasted_iota(jnp.int32, hb.shape, 0)
    hb = jnp.where(row == 0, 0.0, hb)
    _store_lane_split(sig_ref, hf + hb)
    xs = _dft_forward(sig_ref, wdft_ref)
    for k in range(RADIX):
        kre_ref[k * SUB:(k + 1) * SUB, :] = xs[k][0]
    _store_lane_split(sig_ref, hf - hb)
    xd = _dft_forward(sig_ref, wdft_ref)
    for k in range(RADIX):
        kim_ref[k * SUB:(k + 1) * SUB, :] = xd[k][1]


def _filter_spectrum(w1p, b1, w2, b2, w3, b3, fr, w4, feat, decay, wdft):
    nct = HYENA_WIDTH // CT
    full = lambda shape: pl.BlockSpec(shape, lambda c: (0,) * len(shape))
    return pl.pallas_call(
        _filter_kernel,
        grid=(nct,),
        in_specs=[
            full((SEQ, FEAT_PAD)), full((FEAT_PAD, FILTER_HIDDEN)), full((1, FILTER_HIDDEN)),
            full((FILTER_HIDDEN, FILTER_HIDDEN)), full((1, FILTER_HIDDEN)),
            full((FILTER_HIDDEN, FILTER_HIDDEN)), full((1, FILTER_HIDDEN)), full((1, FILTER_HIDDEN)),
            pl.BlockSpec((FILTER_HIDDEN, CT), lambda c: (0, c)),
            pl.BlockSpec((FILTER_HIDDEN, CT), lambda c: (0, nct + c)),
            pl.BlockSpec((SEQ, CT), lambda c: (0, c)),
            full((RADIX, 2 * SUB, SUB)),
        ],
        out_specs=[pl.BlockSpec((SEQ, CT), lambda c: (0, c)), pl.BlockSpec((SEQ, CT), lambda c: (0, c))],
        out_shape=[jax.ShapeDtypeStruct((SEQ, HYENA_WIDTH), f32)] * 2,
        scratch_shapes=[pltpu.VMEM((SEQ, FILTER_HIDDEN), f32), pltpu.VMEM((CT // LANE, SEQ, LANE), f32)],
        compiler_params=pltpu.CompilerParams(dimension_semantics=("arbitrary",),
                                             vmem_limit_bytes=VMEM_LIMIT),
        name="filter_spectrum",
    )(feat, w1p, b1, w2, b2, w3, b3, fr, w4, w4, decay, wdft)


def _short_conv(u, cw_ref, cb_ref, pad_ref):
    pad_ref[8:8 + SEQ, :] = u
    cw = cw_ref[...]
    return (cb_ref[...] + cw[0:1] * pad_ref[7:7 + SEQ, :] + cw[1:2] * u + cw[2:3] * pad_ref[9:9 + SEQ, :])


def _hyena_kernel(x_ref, wx0_ref, wx1_ref, wv_ref, wg_ref,
                  cw0_ref, cw1_ref, cwv_ref, cb0_ref, cb1_ref, cbv_ref, skip_ref,
                  kre_ref, kim_ref, wdft_ref, vdft_ref, o_ref,
                  pad_ref, z_ref, y_ref, gate_ref):
    zeros8 = jnp.zeros((8, CT), f32)
    pad_ref[0:8, :] = zeros8
    pad_ref[8 + SEQ:16 + SEQ, :] = zeros8
    x = x_ref[...]
    proj = lambda w_ref: jnp.dot(x, w_ref[...], preferred_element_type=f32)

    x1 = _short_conv(proj(wx1_ref), cw1_ref, cb1_ref, pad_ref)
    hv = _short_conv(proj(wv_ref), cwv_ref, cbv_ref, pad_ref)
    _store_lane_split(z_ref, hv * x1)
    x0 =_short_conv(proj(wx0_ref), cw0_ref, cb0_ref, pad_ref)
    g = proj(wg_ref)
    gate_ref[...] = x0 * (g / (1.0 + jnp.exp(-g)))

    xs = _dft_forward(z_ref, wdft_ref)
    p = []
    for k in range(RADIX):
        kre = kre_ref[k * SUB:(k + 1) * SUB, :]
        kim = kim_ref[k * SUB:(k + 1) * SUB, :]
        xr, xi = xs[k]
        p.append((xr * kre - xi * kim, xr * kim + xi * kre))
    _dft_inverse(p, vdft_ref, y_ref)
    y = _load_lane_split(y_ref)
    z = _load_lane_split(z_ref)
    o_ref[...] = ((y + z * skip_ref[...]) * gate_ref[...]).astype(o_ref.dtype)


def _hyena_branch(xb, w_in_b, conv_w, conv_b, skip, kre, kim, wdft, vdft):
    batch = xb.shape[0]
    nct = HYENA_WIDTH // CT
    wcol = lambda base: pl.BlockSpec((D_MODEL, CT), lambda c, b: (0, base // CT + c))
    ccol = lambda rows, part: pl.BlockSpec((rows, CT), lambda c, b: (0, part * nct + c))
    tile = pl.BlockSpec((SEQ, CT), lambda c, b: (0, c))
    return pl.pallas_call(
        _hyena_kernel,
        grid=(nct, batch),
        in_specs=[
            pl.BlockSpec((None, SEQ, D_MODEL), lambda c, b: (b, 0, 0)),
            wcol(COL_HY), wcol(COL_HY + HYENA_WIDTH), wcol(COL_HY + 2 * HYENA_WIDTH), wcol(COL_HGATE),
            ccol(3, 0), ccol(3, 1), ccol(3, 2), ccol(1, 0), ccol(1, 1), ccol(1, 2),
            pl.BlockSpec((1, CT), lambda c, b: (0, c)),
            tile, tile,
            pl.BlockSpec((RADIX, 2 * SUB, SUB), lambda c, b: (0, 0, 0)),
            pl.BlockSpec((RADIX, SUB, 2 * SUB), lambda c, b: (0, 0, 0)),
        ],
        out_specs=pl.BlockSpec((None, SEQ, CT), lambda c, b: (b, 0, c)),
        out_shape=jax.ShapeDtypeStruct((batch, SEQ, HYENA_WIDTH), bf16),
        scratch_shapes=[pltpu.VMEM((SEQ + 16, CT), f32), pltpu.VMEM((CT // LANE, SEQ, LANE), f32),
                        pltpu.VMEM((CT // LANE, SEQ, LANE), f32), pltpu.VMEM((SEQ, CT), f32)],
        compiler_params=pltpu.CompilerParams(dimension_semantics=("arbitrary", "arbitrary"),
                                             vmem_limit_bytes=VMEM_LIMIT),
        name="hyena_branch",
    )(xb, w_in_b, w_in_b, w_in_b, w_in_b, conv_w, conv_w, conv_w, conv_b, conv_b, conv_b,
      skip, kre, kim, wdft, vdft)


ATT_COLS = 2 * ATTN_WIDTH + 2 * KV_WIDTH
PROJ_TM = 1024
PROJ_TN = 512


def _attn_proj_kernel(x_ref, w_ref, o_ref):
    x = x_ref[...]
    scale = HEAD_DIM ** -0.5
    for j in range(ATT_COLS // PROJ_TN):
        cols = slice(j * PROJ_TN, (j + 1) * PROJ_TN)
        acc = jnp.dot(x, w_ref[:, cols], preferred_element_type=f32)
        if (j + 1) * PROJ_TN <= ATTN_WIDTH:
            acc = acc * scale
        o_ref[:, cols] = acc.astype(o_ref.dtype)


def _attn_proj(xb2d, w_att):
    rows = xb2d.shape[0]
    return pl.pallas_call(
        _attn_proj_kernel,
        grid=(rows // PROJ_TM,),
        in_specs=[pl.BlockSpec((PROJ_TM, D_MODEL), lambda i: (i, 0)),
                  pl.BlockSpec((D_MODEL, ATT_COLS), lambda i: (0, 0))],
        out_specs=pl.BlockSpec((PROJ_TM, ATT_COLS), lambda i: (i, 0)),
        out_shape=jax.ShapeDtypeStruct((rows, ATT_COLS), bf16),
        compiler_params=pltpu.CompilerParams(dimension_semantics=("arbitrary",),
                                             vmem_limit_bytes=VMEM_LIMIT),
        name="attn_proj",
    )(xb2d, w_att)


def _attn_kernel(relb_ref, sink_ref, bucket_ref, q_ref, g_ref, kv_ref, o_ref, bias_ref):
    n = pl.program_id(1)
    nb = pl.num_programs(1)

    @pl.when((pl.program_id(0) == 0) & (n == 0))
    def _():
        bucket = bucket_ref[...]
        for h in range(N_HEADS):
            acc = jnp.full(bucket.shape, NEG_INF, f32)
            for kb in range(REL_BUCKETS):
                acc = jnp.where(bucket == kb, relb_ref[kb, h], acc)
            bias_ref[h] = acc

    start = pl.multiple_of(n * BLOCK, BLOCK)
    prev = pl.multiple_of(jnp.maximum(n - 1, 0) * BLOCK, BLOCK)
    nxt = pl.multiple_of(jnp.minimum(n + 1, nb - 1) * BLOCK, BLOCK)
    col = lax.broadcasted_iota(jnp.int32, (BLOCK, 3 * BLOCK), 1)
    lo = jnp.where(n == 0, BLOCK, 0)
    hi = jnp.where(n == nb - 1, 2 * BLOCK, 3 * BLOCK)
    in_seq = (col >= lo) & (col < hi)

    for h in range(N_KV_HEADS):
        kc = slice(h * HEAD_DIM, (h + 1) * HEAD_DIM)
        vc = slice(KV_WIDTH + h * HEAD_DIM, KV_WIDTH + (h + 1) * HEAD_DIM)
        kband = jnp.concatenate([kv_ref[pl.ds(prev, BLOCK), kc], kv_ref[pl.ds(start, BLOCK), kc],
                                 kv_ref[pl.ds(nxt, BLOCK), kc]], axis=0)
        vband = jnp.concatenate([kv_ref[pl.ds(prev, BLOCK), vc], kv_ref[pl.ds(start, BLOCK), vc],
                                 kv_ref[pl.ds(nxt, BLOCK), vc]], axis=0)
        qs = jnp.concatenate(
            [q_ref[:, (h * GROUP + g) * HEAD_DIM:(h * GROUP + g + 1) * HEAD_DIM] for g in range(GROUP)], axis=0)
        sc_all = lax.dot_general(qs, kband, (((1,), (1,)), ((), ())), preferred_element_type=f32)
        ps, dens = [], []
        for g in range(GROUP):
            head = h * GROUP + g
            sink = sink_ref[head]
            sc = sc_all[g * BLOCK:(g + 1) * BLOCK] + bias_ref[head]
            sc = jnp.where(in_seq, sc, NEG_INF)
            m = jnp.maximum(jnp.max(sc, axis=-1, keepdims=True), sink)
            e = jnp.exp(sc - m)
            dens.append(jnp.sum(e, axis=-1, keepdims=True) + jnp.exp(sink - m))
            ps.append(e.astype(bf16))
        o_all = jnp.dot(jnp.concatenate(ps, axis=0), vband, preferred_element_type=f32)
        for g in range(GROUP):
            head = h * GROUP + g
            hc = slice(head * HEAD_DIM, (head + 1) * HEAD_DIM)
            gate = g_ref[:, hc].astype(f32)
            y = o_all[g * BLOCK:(g + 1) * BLOCK] / dens[g]
            o_ref[:, hc] = (y * (gate / (1.0 + jnp.exp(-gate)))).astype(o_ref.dtype)


def _attn_branch(u_att, rel_bias, sink, bucket, batch):
    nb = SEQ // BLOCK
    return pl.pallas_call(
        _attn_kernel,
        grid=(batch, nb),
        in_specs=[
            pl.BlockSpec(memory_space=pltpu.SMEM),
            pl.BlockSpec(memory_space=pltpu.SMEM),
            pl.BlockSpec((BLOCK, 3 * BLOCK), lambda b, n: (0, 0)),
            pl.BlockSpec((BLOCK, ATTN_WIDTH), lambda b, n: (b * nb + n, 0)),
            pl.BlockSpec((BLOCK, ATTN_WIDTH), lambda b, n: (b * nb + n, 1)),
            pl.BlockSpec((SEQ, 2 * KV_WIDTH), lambda b, n: (b, 2 * ATTN_WIDTH // (2 * KV_WIDTH))),
        ],
        out_specs=pl.BlockSpec((BLOCK, ATTN_WIDTH), lambda b, n: (b * nb + n, 0)),
        out_shape=jax.ShapeDtypeStruct((batch * SEQ, ATTN_WIDTH), bf16),
        scratch_shapes=[pltpu.VMEM((N_HEADS, BLOCK, 3 * BLOCK), f32)],
        compiler_params=pltpu.CompilerParams(dimension_semantics=("arbitrary", "arbitrary"),
                                             vmem_limit_bytes=VMEM_LIMIT),
        name="attn_branch",
    )(rel_bias, sink, bucket, u_att, u_att, u_att)


OUT_TM = 512


def _merge_kernel(x_ref, ya_ref, yh_ref, wbr_ref, wa_ref, wh_ref, wo_ref, g_ref, b_ref, o_ref):
    x = x_ref[...]
    br = jnp.dot(x.astype(bf16), wbr_ref[...], preferred_element_type=f32)
    gates = 1.0 / (1.0 + jnp.exp(-br))
    pa = jnp.dot(ya_ref[...], wa_ref[...], preferred_element_type=f32)
    ph = jnp.dot(yh_ref[...], wh_ref[...], preferred_element_type=f32)
    merged = gates[:, :D_MODEL] * pa + gates[:, D_MODEL:] * ph
    out = jnp.dot(merged.astype(bf16), wo_ref[...], preferred_element_type=f32)
    r = DEEPNORM_ALPHA * x + out
    mu = jnp.mean(r, axis=-1, keepdims=True)
    d = r - mu
    var = jnp.mean(d * d, axis=-1, keepdims=True)
    o_ref[...] = d * lax.rsqrt(var + LN_EPS) * g_ref[...] + b_ref[...]


def _merge_out(x2d, ya, yh, wbr, wa, wh, wo, ln_g, ln_b):
    rows = x2d.shape[0]
    row = lambda width: pl.BlockSpec((OUT_TM, width), lambda i: (i, 0))
    full = lambda shape: pl.BlockSpec(shape, lambda i: (0, 0))
    return pl.pallas_call(
        _merge_kernel,
        grid=(rows // OUT_TM,),
        in_specs=[row(D_MODEL), row(ATTN_WIDTH), row(HYENA_WIDTH),
                  full((D_MODEL, 2 * D_MODEL)), full((ATTN_WIDTH, D_MODEL)), full((HYENA_WIDTH, D_MODEL)),
                  full((D_MODEL, D_MODEL)), full((1, D_MODEL)), full((1, D_MODEL))],
        out_specs=row(D_MODEL),
        out_shape=jax.ShapeDtypeStruct((rows, D_MODEL), f32),
        compiler_params=pltpu.CompilerParams(dimension_semantics=("arbitrary",),
                                             vmem_limit_bytes=VMEM_LIMIT),
        name="merge_out",
    )(x2d, ya, yh, wbr, wa, wh, wo, ln_g, ln_b)


def _layer(x, w_in, rel_bias, attn_sink, conv_w, conv_b, filt_w1, filt_b1, filt_w2, filt_b2,
           filt_w3, filt_b3, filt_w4, filt_freq, hyena_skip, w_branch_attn, w_branch_hyena,
           w_out, ln_g, ln_b):
    batch = x.shape[0]
    wdft_np, vdft_np = _dft_matrices()
    feat_np, decay_np = _filter_constants()
    wdft = jnp.asarray(wdft_np, dtype=bf16)
    vdft = jnp.asarray(vdft_np, dtype=bf16)
    row2d = lambda v: v.reshape(1, -1).astype(f32)

    w1p = jnp.pad(filt_w1.astype(f32), ((0, FEAT_PAD - FILTER_EMB), (0, 0)))
    kre, kim = _filter_spectrum(w1p, row2d(filt_b1), filt_w2.astype(f32), row2d(filt_b2),
                                filt_w3.astype(f32), row2d(filt_b3), row2d(filt_freq),
                                filt_w4.astype(f32), jnp.asarray(feat_np), jnp.asarray(decay_np), wdft)

    xb = x.astype(bf16)
    w_in_b = w_in.astype(bf16)
    y_h = _hyena_branch(xb, w_in_b, conv_w.astype(f32), row2d(conv_b), row2d(hyena_skip),
                        kre, kim, wdft, vdft)

    w_att = jnp.concatenate([w_in_b[:, COL_Q:COL_K], w_in_b[:, COL_AGATE:COL_HY],
                             w_in_b[:, COL_K:COL_AGATE]], axis=1)
    u_att = _attn_proj(xb.reshape(batch * SEQ, D_MODEL), w_att)
    y_a = _attn_branch(u_att, rel_bias.astype(f32), attn_sink.astype(f32),
                       jnp.asarray(_bucket_table()), batch)

    out = _merge_out(x.reshape(batch * SEQ, D_MODEL), y_a, y_h.reshape(batch * SEQ, HYENA_WIDTH),
                     w_in_b[:, COL_BR:], w_branch_attn.astype(bf16), w_branch_hyena.astype(bf16),
                     w_out.astype(bf16), row2d(ln_g), row2d(ln_b))
    return out.reshape(x.shape).astype(x.dtype)


def kernel(x, w_in, rel_bias, attn_sink, conv_w, conv_b, filt_w1, filt_b1, filt_w2, filt_b2,
           filt_w3, filt_b3, filt_w4, filt_freq, hyena_skip, w_branch_attn, w_branch_hyena,
           w_out, ln_g, ln_b):
    h = x
    for l in range(DEPTH):
        h = _layer(h, w_in[l], rel_bias, attn_sink[l], conv_w[l], conv_b[l],
                   filt_w1[l], filt_b1[l], filt_w2[l], filt_b2[l], filt_w3[l], filt_b3[l],
                   filt_w4[l], filt_freq[l], hyena_skip[l], w_branch_attn[l],
                   w_branch_hyena[l], w_out[l], ln_g[l], ln_b[l])
    return h
```

```python
import functools
import math

import numpy as np
import jax
import jax.numpy as jnp
from jax import lax
from jax.experimental import pallas as pl
from jax.experimental.pallas import tpu as pltpu

D_MODEL = 1024
SEQ = 2048
N_HEADS = 16
N_KV_HEADS = 4
HEAD_DIM = 64
GROUP = N_HEADS // N_KV_HEADS
ATTN_WIDTH = N_HEADS * HEAD_DIM
KV_WIDTH = N_KV_HEADS * HEAD_DIM
WINDOW = 128
BLOCK = 128
REL_BUCKETS = 32
REL_MAX_DIST = 128
NEG_INF = -1e30
HYENA_WIDTH = 1024
FILTER_EMB = 33
FILTER_HIDDEN = 64
DECAY_FAST = 0.3
DECAY_SLOW = 1.5
DECAY_TARGET = 1e-2
DEPTH = 1
DEEPNORM_ALPHA = (2 * DEPTH) ** 0.25
LN_EPS = 1e-5

COL_Q = 0
COL_K = ATTN_WIDTH
COL_V = COL_K + KV_WIDTH
COL_AGATE = COL_V + KV_WIDTH
COL_HY = COL_AGATE + ATTN_WIDTH
COL_HGATE = COL_HY + 3 * HYENA_WIDTH
COL_BR = COL_HGATE + HYENA_WIDTH
IN_COLS = COL_BR + 2 * D_MODEL

DFT_N = 2 * SEQ
RADIX = 4
SUB = SEQ // RADIX
CT = 256
LANE = 128
FEAT_PAD = 128

VMEM_LIMIT = 56 * 1024 * 1024

f32 = jnp.float32
bf16 = jnp.bfloat16


@functools.lru_cache(maxsize=None)
def _dft_matrices():
    f = np.arange(SUB, dtype=np.int64)[:, None]
    m = np.arange(SUB, dtype=np.int64)[None, :]
    w = np.empty((RADIX, 2 * SUB, SUB), np.float32)
    for r in range(RADIX):
        idx = ((2 * f + 1) * (RADIX * m + r)) % (2 * DFT_N)
        ang = np.pi * idx.astype(np.float64) / DFT_N
        w[r, :SUB] = np.cos(ang)
        w[r, SUB:] = -np.sin(ang)
    v = np.ascontiguousarray(np.transpose(w, (0, 2, 1))) * np.float32(2.0 / DFT_N)
    return w, v


@functools.lru_cache(maxsize=None)
def _filter_constants():
    bands = (FILTER_EMB - 1) // 2
    t = np.linspace(0.0, 1.0, SEQ, dtype=np.float32)[:, None]
    w = (2.0 * math.pi * np.arange(SEQ, dtype=np.float32)[:, None] / SEQ).astype(np.float32)
    fb = np.linspace(1e-4, bands - 1, bands, dtype=np.float32)[None, :]
    arg = (fb * w).astype(np.float64)
    feat = np.zeros((SEQ, FEAT_PAD), np.float32)
    feat[:, 0:1] = t
    feat[:, 1:1 + bands] = np.cos(arg)
    feat[:, 1 + bands:1 + 2 * bands] = -np.sin(arg)
    max_decay = math.log(DECAY_TARGET) / DECAY_FAST
    min_decay = math.log(DECAY_TARGET) / DECAY_SLOW
    deltas = np.linspace(min_decay, max_decay, HYENA_WIDTH, dtype=np.float32)
    decay = np.exp(-t.astype(np.float64) * np.abs(deltas.astype(np.float64))).astype(np.float32)
    return feat, decay


@functools.lru_cache(maxsize=None)
def _bucket_table():
    a = np.arange(BLOCK)[:, None]
    c = np.arange(3 * BLOCK)[None, :]
    rel = c - BLOCK - a
    half = REL_BUCKETS // 2
    max_exact = half // 2
    ret = (rel > 0).astype(np.int32) * half
    n = np.abs(rel)
    n_safe = np.maximum(n, 1).astype(np.float32)
    large = max_exact + (np.log(n_safe / max_exact) / math.log(REL_MAX_DIST / max_exact)
                         * (half - max_exact)).astype(np.int32)
    large = np.minimum(large, half - 1)
    bucket = (ret + np.where(n < max_exact, n, large)).astype(np.int32)
    return np.ascontiguousarray(np.where(np.abs(rel) <= WINDOW, bucket, -1).astype(np.int32).T)


def _store_lane_split(ref, val):
    for j in range(ref.shape[0]):
        ref[j] = val[:, j * LANE:(j + 1) * LANE]


def _load_lane_split(ref, rows=slice(None)):
    return jnp.concatenate([ref[j, rows, :] for j in range(ref.shape[0])], axis=1)


def _dft_forward(z_ref, w_ref):
    a = []
    for r in range(RADIX):
        zr = _load_lane_split(z_ref, pl.ds(r, SUB, stride=RADIX)).astype(bf16)
        ar = jnp.dot(w_ref[r], zr, preferred_element_type=f32)
        a.append((ar[:SUB], ar[SUB:]))
    s02 = (a[0][0] + a[2][0], a[0][1] + a[2][1])
    d02 = (a[0][0] - a[2][0], a[0][1] - a[2][1])
    s13 = (a[1][0] + a[3][0], a[1][1] + a[3][1])
    d13 = (a[1][0] - a[3][0], a[1][1] - a[3][1])
    x0 = (s02[0] + s13[0], s02[1] + s13[1])
    x2 = (s02[0] - s13[0], s02[1] - s13[1])
    x1 = (d02[0] + d13[1], d02[1] - d13[0])
    x3 = (d02[0] - d13[1], d02[1] + d13[0])
    return [x0, x1, x2, x3]


def _dft_inverse(p, v_ref, y_ref):
    s02 = (p[0][0] + p[2][0], p[0][1] + p[2][1])
    d02 = (p[0][0] - p[2][0], p[0][1] - p[2][1])
    s13 = (p[1][0] + p[3][0], p[1][1] + p[3][1])
    d13 = (p[1][0] - p[3][0], p[1][1] - p[3][1])
    q = [
        (s02[0] + s13[0], s02[1] + s13[1]),
        (d02[0] - d13[1], d02[1] + d13[0]),
        (s02[0] - s13[0], s02[1] - s13[1]),
        (d02[0] + d13[1], d02[1] - d13[0]),
    ]
    for r in range(RADIX):
        qr = jnp.concatenate([q[r][0], q[r][1]], axis=0).astype(bf16)
        yr = jnp.dot(v_ref[r], qr, preferred_element_type=f32)
        for j in range(y_ref.shape[0]):
            y_ref[j, pl.ds(r, SUB, stride=RADIX), :] = yr[:, j * LANE:(j + 1) * LANE]


def _filter_kernel(feat_ref, w1_ref, b1_ref, w2_ref, b2_ref, w3_ref, b3_ref, fr_ref,
                   w4f_ref, w4b_ref, decay_ref, wdft_ref, kre_ref, kim_ref, h3_ref, sig_ref):
    hi = lax.Precision.HIGHEST

    @pl.when(pl.program_id(0) == 0)
    def _():
        fr = fr_ref[...]
        h = jnp.sin(fr * (jnp.dot(feat_ref[...], w1_ref[...], precision=hi,
                                  preferred_element_type=f32) + b1_ref[...]))
        h = jnp.sin(fr * (jnp.dot(h, w2_ref[...], precision=hi, preferred_element_type=f32) + b2_ref[...]))
        h = jnp.sin(fr * (jnp.dot(h, w3_ref[...], precision=hi, preferred_element_type=f32) + b3_ref[...]))
        h3_ref[...] = h

    h3 = h3_ref[...]
    decay = decay_ref[...]
    hf = jnp.dot(h3, w4f_ref[...], precision=hi, preferred_element_type=f32) * decay
    hb = jnp.dot(h3, w4b_ref[...], precision=hi, preferred_element_type=f32) * decay
    row = lax.broadcasted_iota(jnp.int32, hb.shape, 0)
    hb = jnp.where(row == 0, 0.0, hb)
    _store_lane_split(sig_ref, hf + hb)
    xs = _dft_forward(sig_ref, wdft_ref)
    for k in range(RADIX):
        kre_ref[k * SUB:(k + 1) * SUB, :] = xs[k][0]
    _store_lane_split(sig_ref, hf - hb)
    xd = _dft_forward(sig_ref, wdft_ref)
    for k in range(RADIX):
        kim_ref[k * SUB:(k + 1) * SUB, :] = xd[k][1]


def _filter_spectrum(w1p, b1, w2, b2, w3, b3, fr, w4, feat, decay, wdft):
    nct = HYENA_WIDTH // CT
    full = lambda shape: pl.BlockSpec(shape, lambda c: (0,) * len(shape))
    return pl.pallas_call(
        _filter_kernel,
        grid=(nct,),
        in_specs=[
            full((SEQ, FEAT_PAD)), full((FEAT_PAD, FILTER_HIDDEN)), full((1, FILTER_HIDDEN)),
            full((FILTER_HIDDEN, FILTER_HIDDEN)), full((1, FILTER_HIDDEN)),
            full((FILTER_HIDDEN, FILTER_HIDDEN)), full((1, FILTER_HIDDEN)), full((1, FILTER_HIDDEN)),
            pl.BlockSpec((FILTER_HIDDEN, CT), lambda c: (0, c)),
            pl.BlockSpec((FILTER_HIDDEN, CT), lambda c: (0, nct + c)),
            pl.BlockSpec((SEQ, CT), lambda c: (0, c)),
            full((RADIX, 2 * SUB, SUB)),
        ],
        out_specs=[pl.BlockSpec((SEQ, CT), lambda c: (0, c)), pl.BlockSpec((SEQ, CT), lambda c: (0, c))],
        out_shape=[jax.ShapeDtypeStruct((SEQ, HYENA_WIDTH), f32)] * 2,
        scratch_shapes=[pltpu.VMEM((SEQ, FILTER_HIDDEN), f32), pltpu.VMEM((CT // LANE, SEQ, LANE), f32)],
        compiler_params=pltpu.CompilerParams(dimension_semantics=("arbitrary",),
                                             vmem_limit_bytes=VMEM_LIMIT),
        name="filter_spectrum",
    )(feat, w1p, b1, w2, b2, w3, b3, fr, w4, w4, decay, wdft)


def _short_conv(u, cw_ref, cb_ref, pad_ref):
    pad_ref[8:8 + SEQ, :] = u
    cw = cw_ref[...]
    return (cb_ref[...] + cw[0:1] * pad_ref[7:7 + SEQ, :] + cw[1:2] * u + cw[2:3] * pad_ref[9:9 + SEQ, :])


def _hyena_kernel(x_ref, wx0_ref, wx1_ref, wv_ref, wg_ref,
                  cw0_ref, cw1_ref, cwv_ref, cb0_ref, cb1_ref, cbv_ref, skip_ref,
                  kre_ref, kim_ref, wdft_ref, vdft_ref, o_ref,
                  pad_ref, z_ref, y_ref, gate_ref):
    zeros8 = jnp.zeros((8, CT), f32)
    pad_ref[0:8, :] = zeros8
    pad_ref[8 + SEQ:16 + SEQ, :] = zeros8
    x = x_ref[...]
    proj = lambda w_ref: jnp.dot(x, w_ref[...], preferred_element_type=f32)

    x1 = _short_conv(proj(wx1_ref), cw1_ref, cb1_ref, pad_ref)
    hv = _short_conv(proj(wv_ref), cwv_ref, cbv_ref, pad_ref)
    _store_lane_split(z_ref, hv * x1)
    x0 =_short_conv(proj(wx0_ref), cw0_ref, cb0_ref, pad_ref)
    g = proj(wg_ref)
    gate_ref[...] = x0 * (g / (1.0 + jnp.exp(-g)))

    xs = _dft_forward(z_ref, wdft_ref)
    p = []
    for k in range(RADIX):
        kre = kre_ref[k * SUB:(k + 1) * SUB, :]
        kim = kim_ref[k * SUB:(k + 1) * SUB, :]
        xr, xi = xs[k]
        p.append((xr * kre - xi * kim, xr * kim + xi * kre))
    _dft_inverse(p, vdft_ref, y_ref)
    y = _load_lane_split(y_ref)
    z = _load_lane_split(z_ref)
    o_ref[...] = ((y + z * skip_ref[...]) * gate_ref[...]).astype(o_ref.dtype)


def _hyena_branch(xb, w_in_b, conv_w, conv_b, skip, kre, kim, wdft, vdft):
    batch = xb.shape[0]
    nct = HYENA_WIDTH // CT
    wcol = lambda base: pl.BlockSpec((D_MODEL, CT), lambda c, b: (0, base // CT + c))
    ccol = lambda rows, part: pl.BlockSpec((rows, CT), lambda c, b: (0, part * nct + c))
    tile = pl.BlockSpec((SEQ, CT), lambda c, b: (0, c))
    return pl.pallas_call(
        _hyena_kernel,
        grid=(nct, batch),
        in_specs=[
            pl.BlockSpec((None, SEQ, D_MODEL), lambda c, b: (b, 0, 0)),
            wcol(COL_HY), wcol(COL_HY + HYENA_WIDTH), wcol(COL_HY + 2 * HYENA_WIDTH), wcol(COL_HGATE),
            ccol(3, 0), ccol(3, 1), ccol(3, 2), ccol(1, 0), ccol(1, 1), ccol(1, 2),
            pl.BlockSpec((1, CT), lambda c, b: (0, c)),
            tile, tile,
            pl.BlockSpec((RADIX, 2 * SUB, SUB), lambda c, b: (0, 0, 0)),
            pl.BlockSpec((RADIX, SUB, 2 * SUB), lambda c, b: (0, 0, 0)),
        ],
        out_specs=pl.BlockSpec((None, SEQ, CT), lambda c, b: (b, 0, c)),
        out_shape=jax.ShapeDtypeStruct((batch, SEQ, HYENA_WIDTH), bf16),
        scratch_shapes=[pltpu.VMEM((SEQ + 16, CT), f32), pltpu.VMEM((CT // LANE, SEQ, LANE), f32),
                        pltpu.VMEM((CT // LANE, SEQ, LANE), f32), pltpu.VMEM((SEQ, CT), f32)],
        compiler_params=pltpu.CompilerParams(dimension_semantics=("arbitrary", "arbitrary"),
                                             vmem_limit_bytes=VMEM_LIMIT),
        name="hyena_branch",
    )(xb, w_in_b, w_in_b, w_in_b, w_in_b, conv_w, conv_w, conv_w, conv_b, conv_b, conv_b,
      skip, kre, kim, wdft, vdft)


ATT_COLS = 2 * ATTN_WIDTH + KV_WIDTH
PROJ_TM = 1024
PROJ_TN = 256
LOG2E = math.log2(math.e)


def _attn_proj_kernel(x_ref, w_ref, wvt_ref, o_ref, vt_ref, xb_ref):
    x = x_ref[...].astype(bf16)
    xb_ref[...] = x
    q_scale = HEAD_DIM ** -0.5 * LOG2E
    for j in range(ATT_COLS // PROJ_TN):
        cols = slice(j * PROJ_TN, (j + 1) * PROJ_TN)
        acc = jnp.dot(x, w_ref[:, cols], preferred_element_type=f32)
        if (j + 1) * PROJ_TN <= ATTN_WIDTH:
            acc = acc * q_scale
        o_ref[:, cols] = acc.astype(o_ref.dtype)
    vt = lax.dot_general(wvt_ref[...], x, (((1,), (1,)), ((), ())), preferred_element_type=f32)
    for j in range(PROJ_TM // BLOCK):
        vt_ref[j] = vt[:, j * BLOCK:(j + 1) * BLOCK].astype(vt_ref.dtype)


def _attn_proj(x2d, w_att, wvt):
    rows = x2d.shape[0]
    return pl.pallas_call(
        _attn_proj_kernel,
        grid=(rows // PROJ_TM,),
        in_specs=[pl.BlockSpec((PROJ_TM, D_MODEL), lambda i: (i, 0)),
                  pl.BlockSpec((D_MODEL, ATT_COLS), lambda i: (0, 0)),
                  pl.BlockSpec((KV_WIDTH, D_MODEL), lambda i: (0, 0))],
        out_specs=[pl.BlockSpec((PROJ_TM, ATT_COLS), lambda i: (i, 0)),
                   pl.BlockSpec((PROJ_TM // BLOCK, KV_WIDTH, BLOCK), lambda i: (i, 0, 0)),
                   pl.BlockSpec((PROJ_TM, D_MODEL), lambda i: (i, 0))],
        out_shape=[jax.ShapeDtypeStruct((rows, ATT_COLS), bf16),
                   jax.ShapeDtypeStruct((rows // BLOCK, KV_WIDTH, BLOCK), bf16),
                   jax.ShapeDtypeStruct((rows, D_MODEL), bf16)],
        compiler_params=pltpu.CompilerParams(dimension_semantics=("arbitrary",),
                                             vmem_limit_bytes=VMEM_LIMIT),
        name="attn_proj",
    )(x2d, w_att, wvt)


N_QBLOCKS = SEQ // BLOCK
BAND = 3 * BLOCK
GQ = GROUP * BLOCK


def _attn_kernel(relb_ref, sink_ref, bucket_ref, q_ref, g_ref, k_ref, vt_ref, o_ref, bias_ref):
    @pl.when(pl.program_id(0) == 0)
    def _():
        bucket = bucket_ref[...]
        neg = jnp.full((BLOCK, GQ), NEG_INF, f32)
        for h in range(N_KV_HEADS):
            cols = []
            for g in range(GROUP):
                acc = jnp.full(bucket.shape, NEG_INF, f32)
                for kb in range(REL_BUCKETS):
                    acc = jnp.where(bucket == kb, relb_ref[kb, h * GROUP + g] * LOG2E, acc)
                cols.append(acc)
            tbl = jnp.concatenate(cols, axis=1)
            bias_ref[0, h] = tbl
            bias_ref[1, h] = jnp.concatenate([neg, tbl[BLOCK:]], axis=0)
            bias_ref[2, h] = jnp.concatenate([tbl[:2 * BLOCK], neg], axis=0)

    sink_rows = [
        jnp.concatenate([jnp.full((1, BLOCK), sink_ref[h * GROUP + g] * LOG2E, f32) for g in range(GROUP)], axis=1)
        for h in range(N_KV_HEADS)]

    def qblock(n, carry):
        pj = jnp.maximum(n - 1, 0)
        nj = jnp.minimum(n + 1, N_QBLOCKS - 1)
        start = pl.multiple_of(n * BLOCK, BLOCK)
        prev = pl.multiple_of(pj * BLOCK, BLOCK)
        nxt = pl.multiple_of(nj * BLOCK, BLOCK)
        variant = jnp.where(n == 0, 1, jnp.where(n == N_QBLOCKS - 1, 2, 0))
        rows = pl.ds(start, BLOCK)

        def scores(h):
            hd = slice(h * HEAD_DIM, (h + 1) * HEAD_DIM)
            kband = jnp.concatenate([k_ref[pl.ds(prev, BLOCK), hd], k_ref[rows, hd],
                                     k_ref[pl.ds(nxt, BLOCK), hd]], axis=0)
            qs = jnp.concatenate(
                [q_ref[rows, (h * GROUP + g) * HEAD_DIM:(h * GROUP + g + 1) * HEAD_DIM] for g in range(GROUP)],
                axis=0)
            return lax.dot_general(kband, qs, (((1,), (1,)), ((), ())), preferred_element_type=f32)

        st_next = scores(0)
        for h in range(N_KV_HEADS):
            hd = slice(h * HEAD_DIM, (h + 1) * HEAD_DIM)
            st = st_next + bias_ref[variant, h]
            if h + 1 < N_KV_HEADS:
                st_next = scores(h + 1)
            vtband = jnp.concatenate([vt_ref[pj, hd, :], vt_ref[n, hd, :], vt_ref[nj, hd, :]], axis=1)
            ones = jnp.ones((16, BAND), f32)
            if h + 1 < N_KV_HEADS:
                ones = ones + jnp.minimum(jnp.abs(st_next[:16, :BAND]), 0.0)
            vtband = jnp.concatenate([vtband, ones.astype(bf16)], axis=0)
            m = jnp.maximum(jnp.max(st, axis=0, keepdims=True), sink_rows[h])
            e = jnp.exp2(st - m)
            ot = jnp.dot(vtband, e.astype(bf16), preferred_element_type=f32)
            den = ot[HEAD_DIM:HEAD_DIM + 1] + jnp.exp2(sink_rows[h] - m)
            ot = ot[:HEAD_DIM] * (1.0 / den)
            for p in range(GROUP // 2):
                pair = jnp.concatenate([ot[:, (2 * p) * BLOCK:(2 * p + 1) * BLOCK],
                                        ot[:, (2 * p + 1) * BLOCK:(2 * p + 2) * BLOCK]], axis=0)
                c0 = (h * GROUP + 2 * p) * HEAD_DIM
                gate = g_ref[rows, c0:c0 + 2 * HEAD_DIM].astype(f32)
                o_ref[rows, c0:c0 + 2 * HEAD_DIM] = (pair.T * (gate / (1.0 + jnp.exp(-gate)))).astype(o_ref.dtype)
        return carry

    lax.fori_loop(0, N_QBLOCKS, qblock, 0, unroll=2)


def _attn_branch(u_att, vt, rel_bias, sink, bucket_t, batch):
    return pl.pallas_call(
        _attn_kernel,
        grid=(batch,),
        in_specs=[
            pl.BlockSpec(memory_space=pltpu.SMEM),
            pl.BlockSpec(memory_space=pltpu.SMEM),
            pl.BlockSpec((BAND, BLOCK), lambda b: (0, 0)),
            pl.BlockSpec((SEQ, ATTN_WIDTH), lambda b: (b, 0)),
            pl.BlockSpec((SEQ, ATTN_WIDTH), lambda b: (b, 1)),
            pl.BlockSpec((SEQ, KV_WIDTH), lambda b: (b, 2 * ATTN_WIDTH // KV_WIDTH)),
            pl.BlockSpec((N_QBLOCKS, KV_WIDTH, BLOCK), lambda b: (b, 0, 0)),
        ],
        out_specs=pl.BlockSpec((SEQ, ATTN_WIDTH), lambda b: (b, 0)),
        out_shape=jax.ShapeDtypeStruct((batch * SEQ, ATTN_WIDTH), bf16),
        scratch_shapes=[pltpu.VMEM((3, N_KV_HEADS, BAND, GQ), f32)],
        compiler_params=pltpu.CompilerParams(dimension_semantics=("arbitrary",),
                                             vmem_limit_bytes=VMEM_LIMIT),
        name="attn_branch",
    )(rel_bias, sink, bucket_t, u_att, u_att, u_att, vt)


OUT_TM = 512


def _merge_kernel(x_ref, ya_ref, yh_ref, wbr_ref, wa_ref, wh_ref, wo_ref, g_ref, b_ref, o_ref):
    x = x_ref[...]
    br = jnp.dot(x.astype(bf16), wbr_ref[...], preferred_element_type=f32)
    gates = 1.0 / (1.0 + jnp.exp(-br))
    pa = jnp.dot(ya_ref[...], wa_ref[...], preferred_element_type=f32)
    ph = jnp.dot(yh_ref[...], wh_ref[...], preferred_element_type=f32)
    merged = gates[:, :D_MODEL] * pa + gates[:, D_MODEL:] * ph
    out = jnp.dot(merged.astype(bf16), wo_ref[...], preferred_element_type=f32)
    r = DEEPNORM_ALPHA * x + out
    mu = jnp.mean(r, axis=-1, keepdims=True)
    d = r - mu
    var = jnp.mean(d * d, axis=-1, keepdims=True)
    o_ref[...] = d * lax.rsqrt(var + LN_EPS) * g_ref[...] + b_ref[...]


def _merge_out(x2d, ya, yh, wbr, wa, wh, wo, ln_g, ln_b):
    rows = x2d.shape[0]
    row = lambda width: pl.BlockSpec((OUT_TM, width), lambda i: (i, 0))
    full = lambda shape: pl.BlockSpec(shape, lambda i: (0, 0))
    return pl.pallas_call(
        _merge_kernel,
        grid=(rows // OUT_TM,),
        in_specs=[row(D_MODEL), row(ATTN_WIDTH), row(HYENA_WIDTH),
                  full((D_MODEL, 2 * D_MODEL)), full((ATTN_WIDTH, D_MODEL)), full((HYENA_WIDTH, D_MODEL)),
                  full((D_MODEL, D_MODEL)), full((1, D_MODEL)), full((1, D_MODEL))],
        out_specs=row(D_MODEL),
        out_shape=jax.ShapeDtypeStruct((rows, D_MODEL), f32),
        compiler_params=pltpu.CompilerParams(dimension_semantics=("arbitrary",),
                                             vmem_limit_bytes=VMEM_LIMIT),
        name="merge_out",
    )(x2d, ya, yh, wbr, wa, wh, wo, ln_g, ln_b)


def _layer(x, w_in, rel_bias, attn_sink, conv_w, conv_b, filt_w1, filt_b1, filt_w2, filt_b2,
           filt_w3, filt_b3, filt_w4, filt_freq, hyena_skip, w_branch_attn, w_branch_hyena,
           w_out, ln_g, ln_b):
    batch = x.shape[0]
    wdft_np, vdft_np = _dft_matrices()
    feat_np, decay_np = _filter_constants()
    wdft = jnp.asarray(wdft_np, dtype=bf16)
    vdft = jnp.asarray(vdft_np, dtype=bf16)
    row2d = lambda v: v.reshape(1, -1).astype(f32)

    w1p = jnp.pad(filt_w1.astype(f32), ((0, FEAT_PAD - FILTER_EMB), (0, 0)))
    kre, kim = _filter_spectrum(w1p, row2d(filt_b1), filt_w2.astype(f32), row2d(filt_b2),
                                filt_w3.astype(f32), row2d(filt_b3), row2d(filt_freq),
                                filt_w4.astype(f32), jnp.asarray(feat_np), jnp.asarray(decay_np), wdft)

    w_in_b = w_in.astype(bf16)
    w_att = jnp.concatenate([w_in_b[:, COL_Q:COL_K], w_in_b[:, COL_AGATE:COL_HY],
                             w_in_b[:, COL_K:COL_V]], axis=1)
    wvt = w_in_b[:, COL_V:COL_AGATE].T
    u_att, vt, xb = _attn_proj(x.reshape(batch * SEQ, D_MODEL), w_att, wvt)
    y_a = _attn_branch(u_att, vt, rel_bias.astype(f32), attn_sink.astype(f32),
                       jnp.asarray(_bucket_table()), batch)

    y_h = _hyena_branch(xb.reshape(batch, SEQ, D_MODEL), w_in_b, conv_w.astype(f32), row2d(conv_b),
                        row2d(hyena_skip), kre, kim, wdft, vdft)

    out = _merge_out(x.reshape(batch * SEQ, D_MODEL), y_a, y_h.reshape(batch * SEQ, HYENA_WIDTH),
                     w_in_b[:, COL_BR:], w_branch_attn.astype(bf16), w_branch_hyena.astype(bf16),
                     w_out.astype(bf16), row2d(ln_g), row2d(ln_b))
    return out.reshape(x.shape).astype(x.dtype)


def kernel(x, w_in, rel_bias, attn_sink, conv_w, conv_b, filt_w1, filt_b1, filt_w2, filt_b2,
           filt_w3, filt_b3, filt_w4, filt_freq, hyena_skip, w_branch_attn, w_branch_hyena,
           w_out, ln_g, ln_b):
    h = x
    for l in range(DEPTH):
        h = _layer(h, w_in[l], rel_bias, attn_sink[l], conv_w[l], conv_b[l],
                   filt_w1[l], filt_b1[l], filt_w2[l], filt_b2[l], filt_w3[l], filt_b3[l],
                   filt_w4[l], filt_freq[l], hyena_skip[l], w_branch_attn[l],
                   w_branch_hyena[l], w_out[l], ln_g[l], ln_b[l])
    return h
```

```python
import functools
import math

import numpy as np
import jax
import jax.numpy as jnp
from jax import lax
from jax.experimental import pallas as pl
from jax.experimental.pallas import tpu as pltpu

D_MODEL = 1024
SEQ = 2048
N_HEADS = 16
N_KV_HEADS = 4
HEAD_DIM = 64
GROUP = N_HEADS // N_KV_HEADS
ATTN_WIDTH = N_HEADS * HEAD_DIM
KV_WIDTH = N_KV_HEADS * HEAD_DIM
WINDOW = 128
BLOCK = 128
REL_BUCKETS = 32
REL_MAX_DIST = 128
NEG_INF = -1e30
HYENA_WIDTH = 1024
FILTER_EMB = 33
FILTER_HIDDEN = 64
DECAY_FAST = 0.3
DECAY_SLOW = 1.5
DECAY_TARGET = 1e-2
DEPTH = 1
DEEPNORM_ALPHA = (2 * DEPTH) ** 0.25
LN_EPS = 1e-5

COL_Q = 0
COL_K = ATTN_WIDTH
COL_V = COL_K + KV_WIDTH
COL_AGATE = COL_V + KV_WIDTH
COL_HY = COL_AGATE + ATTN_WIDTH
COL_HGATE = COL_HY + 3 * HYENA_WIDTH
COL_BR = COL_HGATE + HYENA_WIDTH
IN_COLS = COL_BR + 2 * D_MODEL

DFT_N = 2 * SEQ
RADIX = 4
SUB = SEQ // RADIX
CT = 256
LANE = 128
FEAT_PAD = 128

VMEM_LIMIT = 56 * 1024 * 1024

f32 = jnp.float32
bf16 = jnp.bfloat16


@functools.lru_cache(maxsize=None)
def _dft_matrices():
    f = np.arange(SUB, dtype=np.int64)[:, None]
    m = np.arange(SUB, dtype=np.int64)[None, :]
    w = np.empty((RADIX, 2 * SUB, SUB), np.float32)
    for r in range(RADIX):
        idx = ((2 * f + 1) * (RADIX * m + r)) % (2 * DFT_N)
        ang = np.pi * idx.astype(np.float64) / DFT_N
        w[r, :SUB] = np.cos(ang)
        w[r, SUB:] = -np.sin(ang)
    v = np.ascontiguousarray(np.transpose(w, (0, 2, 1))) * np.float32(2.0 / DFT_N)
    return w, v


@functools.lru_cache(maxsize=None)
def _filter_constants():
    bands = (FILTER_EMB - 1) // 2
    t = np.linspace(0.0, 1.0, SEQ, dtype=np.float32)[:, None]
    w = (2.0 * math.pi * np.arange(SEQ, dtype=np.float32)[:, None] / SEQ).astype(np.float32)
    fb = np.linspace(1e-4, bands - 1, bands, dtype=np.float32)[None, :]
    arg = (fb * w).astype(np.float64)
    feat = np.zeros((SEQ, FEAT_PAD), np.float32)
    feat[:, 0:1] = t
    feat[:, 1:1 + bands] = np.cos(arg)
    feat[:, 1 + bands:1 + 2 * bands] = -np.sin(arg)
    max_decay = math.log(DECAY_TARGET) / DECAY_FAST
    min_decay = math.log(DECAY_TARGET) / DECAY_SLOW
    deltas = np.linspace(min_decay, max_decay, HYENA_WIDTH, dtype=np.float32)
    decay = np.exp(-t.astype(np.float64) * np.abs(deltas.astype(np.float64))).astype(np.float32)
    return feat, decay


@functools.lru_cache(maxsize=None)
def _bucket_table():
    a = np.arange(BLOCK)[:, None]
    c = np.arange(3 * BLOCK)[None, :]
    rel = c - BLOCK - a
    half = REL_BUCKETS // 2
    max_exact = half // 2
    ret = (rel > 0).astype(np.int32) * half
    n = np.abs(rel)
    n_safe = np.maximum(n, 1).astype(np.float32)
    large = max_exact + (np.log(n_safe / max_exact) / math.log(REL_MAX_DIST / max_exact)
                         * (half - max_exact)).astype(np.int32)
    large = np.minimum(large, half - 1)
    bucket = (ret + np.where(n < max_exact, n, large)).astype(np.int32)
    return np.ascontiguousarray(np.where(np.abs(rel) <= WINDOW, bucket, -1).astype(np.int32).T)


def _store_lane_split(ref, val):
    for j in range(ref.shape[0]):
        ref[j] = val[:, j * LANE:(j + 1) * LANE]


def _load_lane_split(ref, rows=slice(None)):
    return jnp.concatenate([ref[j, rows, :] for j in range(ref.shape[0])], axis=1)


def _dft_forward(z_ref, w_ref):
    a = []
    for r in range(RADIX):
        zr = _load_lane_split(z_ref, pl.ds(r, SUB, stride=RADIX)).astype(bf16)
        ar = jnp.dot(w_ref[r], zr, preferred_element_type=f32)
        a.append((ar[:SUB], ar[SUB:]))
    s02 = (a[0][0] + a[2][0], a[0][1] + a[2][1])
    d02 = (a[0][0] - a[2][0], a[0][1] - a[2][1])
    s13 = (a[1][0] + a[3][0], a[1][1] + a[3][1])
    d13 = (a[1][0] - a[3][0], a[1][1] - a[3][1])
    x0 = (s02[0] + s13[0], s02[1] + s13[1])
    x2 = (s02[0] - s13[0], s02[1] - s13[1])
    x1 = (d02[0] + d13[1], d02[1] - d13[0])
    x3 = (d02[0] - d13[1], d02[1] + d13[0])
    return [x0, x1, x2, x3]


def _dft_inverse(p, v_ref, y_ref):
    s02 = (p[0][0] + p[2][0], p[0][1] + p[2][1])
    d02 = (p[0][0] - p[2][0], p[0][1] - p[2][1])
    s13 = (p[1][0] + p[3][0], p[1][1] + p[3][1])
    d13 = (p[1][0] - p[3][0], p[1][1] - p[3][1])
    q = [
        (s02[0] + s13[0], s02[1] + s13[1]),
        (d02[0] - d13[1], d02[1] + d13[0]),
        (s02[0] - s13[0], s02[1] - s13[1]),
        (d02[0] + d13[1], d02[1] - d13[0]),
    ]
    for r in range(RADIX):
        qr = jnp.concatenate([q[r][0], q[r][1]], axis=0).astype(bf16)
        yr = jnp.dot(v_ref[r], qr, preferred_element_type=f32)
        for j in range(y_ref.shape[0]):
            y_ref[j, pl.ds(r, SUB, stride=RADIX), :] = yr[:, j * LANE:(j + 1) * LANE]


def _filter_kernel(feat_ref, w1_ref, b1_ref, w2_ref, b2_ref, w3_ref, b3_ref, fr_ref,
                   w4f_ref, w4b_ref, decay_ref, wdft_ref, kre_ref, kim_ref, h3_ref, sig_ref):
    hi = lax.Precision.HIGHEST

    @pl.when(pl.program_id(0) == 0)
    def _():
        fr = fr_ref[...]
        h = jnp.sin(fr * (jnp.dot(feat_ref[...], w1_ref[...], precision=hi,
                                  preferred_element_type=f32) + b1_ref[...]))
        h = jnp.sin(fr * (jnp.dot(h, w2_ref[...], precision=hi, preferred_element_type=f32) + b2_ref[...]))
        h = jnp.sin(fr * (jnp.dot(h, w3_ref[...], precision=hi, preferred_element_type=f32) + b3_ref[...]))
        h3_ref[...] = h

    h3 = h3_ref[...]
    decay = decay_ref[...]
    hf = jnp.dot(h3, w4f_ref[...], precision=hi, preferred_element_type=f32) * decay
    hb = jnp.dot(h3, w4b_ref[...], precision=hi, preferred_element_type=f32) * decay
    row = lax.broadcasted_iota(jnp.int32, hb.shape, 0)
    hb = jnp.where(row == 0, 0.0, hb)
    _store_lane_split(sig_ref, hf + hb)
    xs = _dft_forward(sig_ref, wdft_ref)
    for k in range(RADIX):
        kre_ref[k * SUB:(k + 1) * SUB, :] = xs[k][0]
    _store_lane_split(sig_ref, hf - hb)
    xd = _dft_forward(sig_ref, wdft_ref)
    for k in range(RADIX):
        kim_ref[k * SUB:(k + 1) * SUB, :] = xd[k][1]


def _filter_spectrum(w1p, b1, w2, b2, w3, b3, fr, w4, feat, decay, wdft):
    nct = HYENA_WIDTH // CT
    full = lambda shape: pl.BlockSpec(shape, lambda c: (0,) * len(shape))
    return pl.pallas_call(
        _filter_kernel,
        grid=(nct,),
        in_specs=[
            full((SEQ, FEAT_PAD)), full((FEAT_PAD, FILTER_HIDDEN)), full((1, FILTER_HIDDEN)),
            full((FILTER_HIDDEN, FILTER_HIDDEN)), full((1, FILTER_HIDDEN)),
            full((FILTER_HIDDEN, FILTER_HIDDEN)), full((1, FILTER_HIDDEN)), full((1, FILTER_HIDDEN)),
            pl.BlockSpec((FILTER_HIDDEN, CT), lambda c: (0, c)),
            pl.BlockSpec((FILTER_HIDDEN, CT), lambda c: (0, nct + c)),
            pl.BlockSpec((SEQ, CT), lambda c: (0, c)),
            full((RADIX, 2 * SUB, SUB)),
        ],
        out_specs=[pl.BlockSpec((SEQ, CT), lambda c: (0, c)), pl.BlockSpec((SEQ, CT), lambda c: (0, c))],
        out_shape=[jax.ShapeDtypeStruct((SEQ, HYENA_WIDTH), f32)] * 2,
        scratch_shapes=[pltpu.VMEM((SEQ, FILTER_HIDDEN), f32), pltpu.VMEM((CT // LANE, SEQ, LANE), f32)],
        compiler_params=pltpu.CompilerParams(dimension_semantics=("arbitrary",),
                                             vmem_limit_bytes=VMEM_LIMIT),
        name="filter_spectrum",
    )(feat, w1p, b1, w2, b2, w3, b3, fr, w4, w4, decay, wdft)


CONV_ROWS = 512
HALF_ROWS = CONV_ROWS // 2
PIECE_ROWS = 64
PACK_ROWS = 16
SPEC_ROWS = PACK_ROWS


def _fold_rows(val, acc):
    for k in range(0, val.shape[0], PACK_ROWS):
        piece = val[k:k + PACK_ROWS]
        acc = piece if acc is None else acc + piece
    return acc


def _exact_zero(fold):
    return jnp.minimum(jnp.abs(fold), 0.0)


def _anchor(w, zero):
    head = (w[:PACK_ROWS].astype(f32) + zero).astype(w.dtype)
    return jnp.concatenate([head, w[PACK_ROWS:]], axis=0)


def _fill_neighbours(ext_ref):
    row = lax.broadcasted_iota(jnp.int32, (SUB, CT), 0)
    last = ext_ref[SEQ - 1:SEQ + SUB - 1, :]
    ext_ref[0:SUB, :] = jnp.where(row == 0, 0.0, last)
    first = ext_ref[SUB + 1:2 * SUB + 1, :]
    ext_ref[SUB + SEQ:2 * SUB + SEQ, :] = jnp.where(row == SUB - 1, 0.0, first)


def _short_conv(ext_ref, i0, rows, cw_ref, cb):
    cw = cw_ref[...]
    return (cb + cw[0:1] * ext_ref[pl.ds(i0, rows), :] + cw[1:2] * ext_ref[pl.ds(SUB + i0, rows), :]
            + cw[2:3] * ext_ref[pl.ds(2 * SUB + i0, rows), :])


def _hyena_kernel(x_ref, wx0_ref, wx1_ref, wv_ref, wg_ref,
                  cw0_ref, cw1_ref, cwv_ref, cb0_ref, cb1_ref, cbv_ref, skip_ref,
                  kre_ref, kim_ref, wdft_ref, vdft_ref, o_ref,
                  e1_ref, ev_ref, e0_ref, gate_ref, z_ref, zb_ref, a_ref, q_ref):
    halves = CT // LANE

    e1_ref[SUB:SUB + SEQ, :] = jnp.dot(x_ref[...], wx1_ref[...], preferred_element_type=f32)
    ev_ref[SUB:SUB + SEQ, :] = jnp.dot(x_ref[...], wv_ref[...], preferred_element_type=f32)
    _fill_neighbours(e1_ref)
    _fill_neighbours(ev_ref)

    def conv_z(i0):
        fold = None
        for c in range(0, HALF_ROWS, PIECE_ROWS):
            z = (_short_conv(ev_ref, i0 + c, PIECE_ROWS, cwv_ref, cbv_ref[...])
                 * _short_conv(e1_ref, i0 + c, PIECE_ROWS, cw1_ref, cb1_ref[...]))
            z_ref[pl.ds(i0 + c, PIECE_ROWS), :] = z
            zb_ref[pl.ds(i0 + c, PIECE_ROWS), :] = z.astype(bf16)
            fold = _fold_rows(z, fold)
        return _exact_zero(fold)

    def proj_and_conv(j, vpu_done):
        for half in range(CONV_ROWS // HALF_ROWS):
            i0 = j * CONV_ROWS + half * HALF_ROWS
            xj = x_ref[pl.ds(i0, HALF_ROWS), :]
            e0 = jnp.dot(xj, _anchor(wx0_ref[...], vpu_done), preferred_element_type=f32)
            g = jnp.dot(xj, _anchor(wg_ref[...], vpu_done), preferred_element_type=f32)
            vpu_done = conv_z(i0)
            e0_ref[pl.ds(SUB + i0, HALF_ROWS), :] = e0
            gate_ref[pl.ds(i0, HALF_ROWS), :] = g
        return vpu_done

    no_dep = jnp.zeros((PACK_ROWS, CT), f32)
    vpu_done = no_dep
    for j in range(SEQ // CONV_ROWS):
        vpu_done = proj_and_conv(j, vpu_done)
    _fill_neighbours(e0_ref)

    def gate_rows(i0):
        fold = None
        for c in range(0, HALF_ROWS, PIECE_ROWS):
            rows = pl.ds(i0 + c, PIECE_ROWS)
            g = gate_ref[rows, :]
            gated = _short_conv(e0_ref, i0 + c, PIECE_ROWS, cw0_ref, cb0_ref[...]) * (g / (1.0 + jnp.exp(-g)))
            gate_ref[rows, :] = gated
            fold = _fold_rows(gated, fold)
        return _exact_zero(fold)

    def dft_and_gate(r, vpu_done):
        i0 = r * SUB
        zr = zb_ref[pl.ds(i0, SUB), :]
        for part in range(2):
            ap = jnp.dot(wdft_ref[r, part * SUB:(part + 1) * SUB, :], _anchor(zr, vpu_done),
                         preferred_element_type=f32)
            vpu_done = gate_rows(i0 + part * HALF_ROWS)
            for j in range(halves):
                a_ref[r * halves + j, part * SUB:(part + 1) * SUB, :] = ap[:, j * LANE:(j + 1) * LANE]
        return vpu_done

    for r in range(RADIX):
        vpu_done = dft_and_gate(r, vpu_done)

    def spectral(t, carry):
        f0 = pl.multiple_of(t * SPEC_ROWS, SPEC_ROWS)

        def a_rows(r, base):
            return jnp.concatenate([a_ref[r * halves + j, pl.ds(base + f0, SPEC_ROWS), :] for j in range(halves)],
                                   axis=1)

        are = [a_rows(r, 0) for r in range(RADIX)]
        aim = [a_rows(r, SUB) for r in range(RADIX)]
        s02 = (are[0] + are[2], aim[0] + aim[2])
        d02 = (are[0] - are[2], aim[0] - aim[2])
        s13 = (are[1] + are[3], aim[1] + aim[3])
        d13 = (are[1] - are[3], aim[1] - aim[3])
        xs = [(s02[0] + s13[0], s02[1] + s13[1]),
              (d02[0] + d13[1], d02[1] - d13[0]),
              (s02[0] - s13[0], s02[1] - s13[1]),
              (d02[0] - d13[1], d02[1] + d13[0])]
        p = []
        for k in range(RADIX):
            kre = kre_ref[pl.ds(k * SUB + f0, SPEC_ROWS), :]
            kim = kim_ref[pl.ds(k * SUB + f0, SPEC_ROWS), :]
            xr, xi = xs[k]
            p.append((xr * kre - xi * kim, xr * kim + xi * kre))
        s02 = (p[0][0] + p[2][0], p[0][1] + p[2][1])
        d02 = (p[0][0] - p[2][0], p[0][1] - p[2][1])
        s13 = (p[1][0] + p[3][0], p[1][1] + p[3][1])
        d13 = (p[1][0] - p[3][0], p[1][1] - p[3][1])
        qs = [(s02[0] + s13[0], s02[1] + s13[1]),
              (d02[0] - d13[1], d02[1] + d13[0]),
              (s02[0] - s13[0], s02[1] - s13[1]),
              (d02[0] + d13[1], d02[1] - d13[0])]
        for r in range(RADIX):
            q_ref[r, pl.ds(f0, SPEC_ROWS), :] = qs[r][0].astype(bf16)
            q_ref[r, pl.ds(SUB + f0, SPEC_ROWS), :] = qs[r][1].astype(bf16)
        return carry

    lax.fori_loop(0, SUB // SPEC_ROWS, spectral, 0)

    half_rows = SUB // 2
    outs = []
    for r in range(RADIX):
        blk = slice(r * SUB, (r + 1) * SUB)
        qr = q_ref[r]
        if r >= 2:
            qr = _anchor(qr, _exact_zero(_fold_rows(outs[r - 2], None)))
        y = jnp.dot(vdft_ref[r], qr, preferred_element_type=f32)
        o = (y + z_ref[blk, :] * skip_ref[...]) * gate_ref[blk, :]
        outs.append(o)
        for j in range(halves):
            for h in range(2):
                a_ref[j * 2 + h, pl.ds(r, half_rows, stride=RADIX), :] = (
                    o[h * half_rows:(h + 1) * half_rows, j * LANE:(j + 1) * LANE])
    o_ref[...] = jnp.concatenate(
        [jnp.concatenate([a_ref[j * 2 + h, 0:SEQ // 2, :] for h in range(2)], axis=0) for j in range(halves)],
        axis=1).astype(o_ref.dtype)


def _hyena_branch(xb, w_in_b, conv_w, conv_b, skip, kre, kim, wdft, vdft):
    batch = xb.shape[0]
    nct = HYENA_WIDTH // CT
    wcol = lambda base: pl.BlockSpec((D_MODEL, CT), lambda c, b: (0, base // CT + c))
    ccol = lambda rows, part: pl.BlockSpec((rows, CT), lambda c, b: (0, part * nct + c))
    tile = pl.BlockSpec((SEQ, CT), lambda c, b: (0, c))
    return pl.pallas_call(
        _hyena_kernel,
        grid=(nct, batch),
        in_specs=[
            pl.BlockSpec((None, SEQ, D_MODEL), lambda c, b: (b, 0, 0)),
            wcol(COL_HY), wcol(COL_HY + HYENA_WIDTH), wcol(COL_HY + 2 * HYENA_WIDTH), wcol(COL_HGATE),
            ccol(3, 0), ccol(3, 1), ccol(3, 2), ccol(1, 0), ccol(1, 1), ccol(1, 2),
            pl.BlockSpec((1, CT), lambda c, b: (0, c)),
            tile, tile,
            pl.BlockSpec((RADIX, 2 * SUB, SUB), lambda c, b: (0, 0, 0)),
            pl.BlockSpec((RADIX, SUB, 2 * SUB), lambda c, b: (0, 0, 0)),
        ],
        out_specs=pl.BlockSpec((None, SEQ, CT), lambda c, b: (b, 0, c)),
        out_shape=jax.ShapeDtypeStruct((batch, SEQ, HYENA_WIDTH), bf16),
        scratch_shapes=[pltpu.VMEM((SEQ + 2 * SUB, CT), f32), pltpu.VMEM((SEQ + 2 * SUB, CT), f32),
                        pltpu.VMEM((SEQ + 2 * SUB, CT), f32), pltpu.VMEM((SEQ, CT), f32),
                        pltpu.VMEM((SEQ, CT), f32), pltpu.VMEM((SEQ, CT), bf16),
                        pltpu.VMEM((RADIX * (CT // LANE), 2 * SUB, LANE), f32),
                        pltpu.VMEM((RADIX, 2 * SUB, CT), bf16)],
        compiler_params=pltpu.CompilerParams(dimension_semantics=("arbitrary", "arbitrary"),
                                             vmem_limit_bytes=VMEM_LIMIT),
        name="hyena_branch",
    )(xb, w_in_b, w_in_b, w_in_b, w_in_b, conv_w, conv_w, conv_w, conv_b, conv_b, conv_b,
      skip, kre, kim, wdft, vdft)


ATT_COLS = 2 * ATTN_WIDTH + KV_WIDTH
PROJ_TM = 1024
PROJ_TN = 256
LOG2E = math.log2(math.e)


def _attn_proj_kernel(x_ref, w_ref, wvt_ref, o_ref, vt_ref, xb_ref, xs_ref):
    for j in range(D_MODEL // LANE):
        xs_ref[j] = x_ref[:, j * LANE:(j + 1) * LANE]
    for r in range(RADIX):
        xb_ref[r] = jnp.concatenate(
            [xs_ref[j, pl.ds(r, PROJ_TM // RADIX, stride=RADIX), :] for j in range(D_MODEL // LANE)],
            axis=1).astype(bf16)
    x = x_ref[...].astype(bf16)
    q_scale = HEAD_DIM ** -0.5 * LOG2E
    for j in range(ATT_COLS // PROJ_TN):
        cols = slice(j * PROJ_TN, (j + 1) * PROJ_TN)
        acc = jnp.dot(x, w_ref[:, cols], preferred_element_type=f32)
        if (j + 1) * PROJ_TN <= ATTN_WIDTH:
            acc = acc * q_scale
        o_ref[:, cols] = acc.astype(o_ref.dtype)
    vt = lax.dot_general(wvt_ref[...], x, (((1,), (1,)), ((), ())), preferred_element_type=f32)
    for j in range(PROJ_TM // BLOCK):
        vt_ref[j] = vt[:, j * BLOCK:(j + 1) * BLOCK].astype(vt_ref.dtype)


def _attn_proj(x2d, w_att, wvt):
    rows = x2d.shape[0]
    tiles_per_seq = SEQ // PROJ_TM
    return pl.pallas_call(
        _attn_proj_kernel,
        grid=(rows // PROJ_TM,),
        in_specs=[pl.BlockSpec((PROJ_TM, D_MODEL), lambda i: (i, 0)),
                  pl.BlockSpec((D_MODEL, ATT_COLS), lambda i: (0, 0)),
                  pl.BlockSpec((KV_WIDTH, D_MODEL), lambda i: (0, 0))],
        out_specs=[pl.BlockSpec((PROJ_TM, ATT_COLS), lambda i: (i, 0)),
                   pl.BlockSpec((PROJ_TM // BLOCK, KV_WIDTH, BLOCK), lambda i: (i, 0, 0)),
                   pl.BlockSpec((None, RADIX, PROJ_TM // RADIX, D_MODEL),
                                lambda i: (i // tiles_per_seq, 0, i % tiles_per_seq, 0))],
        out_shape=[jax.ShapeDtypeStruct((rows, ATT_COLS), bf16),
                   jax.ShapeDtypeStruct((rows // BLOCK, KV_WIDTH, BLOCK), bf16),
                   jax.ShapeDtypeStruct((rows // SEQ, RADIX, SUB, D_MODEL), bf16)],
        scratch_shapes=[pltpu.VMEM((D_MODEL // LANE, PROJ_TM, LANE), f32)],
        compiler_params=pltpu.CompilerParams(dimension_semantics=("arbitrary",),
                                             vmem_limit_bytes=VMEM_LIMIT),
        name="attn_proj",
    )(x2d, w_att, wvt)


N_QBLOCKS = SEQ // BLOCK
BAND = 3 * BLOCK
GQ = GROUP * BLOCK


def _attn_kernel(relb_ref, sink_ref, bucket_ref, q_ref, g_ref, k_ref, vt_ref, o_ref, bias_ref):
    @pl.when(pl.program_id(0) == 0)
    def _():
        bucket = bucket_ref[...]
        neg = jnp.full((BLOCK, GQ), NEG_INF, f32)
        for h in range(N_KV_HEADS):
            cols = []
            for g in range(GROUP):
                acc = jnp.full(bucket.shape, NEG_INF, f32)
                for kb in range(REL_BUCKETS):
                    acc = jnp.where(bucket == kb, relb_ref[kb, h * GROUP + g] * LOG2E, acc)
                cols.append(acc)
            tbl = jnp.concatenate(cols, axis=1)
            bias_ref[0, h] = tbl
            bias_ref[1, h] = jnp.concatenate([neg, tbl[BLOCK:]], axis=0)
            bias_ref[2, h] = jnp.concatenate([tbl[:2 * BLOCK], neg], axis=0)

    sink_rows = [
        jnp.concatenate([jnp.full((1, BLOCK), sink_ref[h * GROUP + g] * LOG2E, f32) for g in range(GROUP)], axis=1)
        for h in range(N_KV_HEADS)]

    def qblock(n, carry):
        pj = jnp.maximum(n - 1, 0)
        nj = jnp.minimum(n + 1, N_QBLOCKS - 1)
        start = pl.multiple_of(n * BLOCK, BLOCK)
        prev = pl.multiple_of(pj * BLOCK, BLOCK)
        nxt = pl.multiple_of(nj * BLOCK, BLOCK)
        variant = jnp.where(n == 0, 1, jnp.where(n == N_QBLOCKS - 1, 2, 0))
        rows = pl.ds(start, BLOCK)

        def scores(h):
            hd = slice(h * HEAD_DIM, (h + 1) * HEAD_DIM)
            kband = jnp.concatenate([k_ref[pl.ds(prev, BLOCK), hd], k_ref[rows, hd],
                                     k_ref[pl.ds(nxt, BLOCK), hd]], axis=0)
            qs = jnp.concatenate(
                [q_ref[rows, (h * GROUP + g) * HEAD_DIM:(h * GROUP + g + 1) * HEAD_DIM] for g in range(GROUP)],
                axis=0)
            return lax.dot_general(kband, qs, (((1,), (1,)), ((), ())), preferred_element_type=f32)

        st_next = scores(0)
        for h in range(N_KV_HEADS):
            hd = slice(h * HEAD_DIM, (h + 1) * HEAD_DIM)
            st = st_next + bias_ref[variant, h]
            if h + 1 < N_KV_HEADS:
                st_next = scores(h + 1)
            vtband = jnp.concatenate([vt_ref[pj, hd, :], vt_ref[n, hd, :], vt_ref[nj, hd, :]], axis=1)
            ones = jnp.ones((16, BAND), f32)
            if h + 1 < N_KV_HEADS:
                ones = ones + jnp.minimum(jnp.abs(st_next[:16, :BAND]), 0.0)
            vtband = jnp.concatenate([vtband, ones.astype(bf16)], axis=0)
            m = jnp.maximum(jnp.max(st, axis=0, keepdims=True), sink_rows[h])
            e = jnp.exp2(st - m)
            ot = jnp.dot(vtband, e.astype(bf16), preferred_element_type=f32)
            den = ot[HEAD_DIM:HEAD_DIM + 1] + jnp.exp2(sink_rows[h] - m)
            ot = ot[:HEAD_DIM] * (1.0 / den)
            for p in range(GROUP // 2):
                pair = jnp.concatenate([ot[:, (2 * p) * BLOCK:(2 * p + 1) * BLOCK],
                                        ot[:, (2 * p + 1) * BLOCK:(2 * p + 2) * BLOCK]], axis=0)
                c0 = (h * GROUP + 2 * p) * HEAD_DIM
                gate = g_ref[rows, c0:c0 + 2 * HEAD_DIM].astype(f32)
                o_ref[rows, c0:c0 + 2 * HEAD_DIM] = (pair.T * (gate / (1.0 + jnp.exp(-gate)))).astype(o_ref.dtype)
        return carry

    lax.fori_loop(0, N_QBLOCKS, qblock, 0, unroll=2)


def _attn_branch(u_att, vt, rel_bias, sink, bucket_t, batch):
    return pl.pallas_call(
        _attn_kernel,
        grid=(batch,),
        in_specs=[
            pl.BlockSpec(memory_space=pltpu.SMEM),
            pl.BlockSpec(memory_space=pltpu.SMEM),
            pl.BlockSpec((BAND, BLOCK), lambda b: (0, 0)),
            pl.BlockSpec((SEQ, ATTN_WIDTH), lambda b: (b, 0)),
            pl.BlockSpec((SEQ, ATTN_WIDTH), lambda b: (b, 1)),
            pl.BlockSpec((SEQ, KV_WIDTH), lambda b: (b, 2 * ATTN_WIDTH // KV_WIDTH)),
            pl.BlockSpec((N_QBLOCKS, KV_WIDTH, BLOCK), lambda b: (b, 0, 0)),
        ],
        out_specs=pl.BlockSpec((SEQ, ATTN_WIDTH), lambda b: (b, 0)),
        out_shape=jax.ShapeDtypeStruct((batch * SEQ, ATTN_WIDTH), bf16),
        scratch_shapes=[pltpu.VMEM((3, N_KV_HEADS, BAND, GQ), f32)],
        compiler_params=pltpu.CompilerParams(dimension_semantics=("arbitrary",),
                                             vmem_limit_bytes=VMEM_LIMIT),
        name="attn_branch",
    )(rel_bias, sink, bucket_t, u_att, u_att, u_att, vt)


OUT_TM = 512


def _merge_kernel(x_ref, ya_ref, yh_ref, wbr_ref, wa_ref, wh_ref, wo_ref, g_ref, b_ref, o_ref):
    x = x_ref[...]
    br = jnp.dot(x.astype(bf16), wbr_ref[...], preferred_element_type=f32)
    gates = 1.0 / (1.0 + jnp.exp(-br))
    pa = jnp.dot(ya_ref[...], wa_ref[...], preferred_element_type=f32)
    ph = jnp.dot(yh_ref[...], wh_ref[...], preferred_element_type=f32)
    merged = gates[:, :D_MODEL] * pa + gates[:, D_MODEL:] * ph
    out = jnp.dot(merged.astype(bf16), wo_ref[...], preferred_element_type=f32)
    r = DEEPNORM_ALPHA * x + out
    mu = jnp.mean(r, axis=-1, keepdims=True)
    d = r - mu
    var = jnp.mean(d * d, axis=-1, keepdims=True)
    o_ref[...] = d * lax.rsqrt(var + LN_EPS) * g_ref[...] + b_ref[...]


def _merge_out(x2d, ya, yh, wbr, wa, wh, wo, ln_g, ln_b):
    rows = x2d.shape[0]
    row = lambda width: pl.BlockSpec((OUT_TM, width), lambda i: (i, 0))
    full = lambda shape: pl.BlockSpec(shape, lambda i: (0, 0))
    return pl.pallas_call(
        _merge_kernel,
        grid=(rows // OUT_TM,),
        in_specs=[row(D_MODEL), row(ATTN_WIDTH), row(HYENA_WIDTH),
                  full((D_MODEL, 2 * D_MODEL)), full((ATTN_WIDTH, D_MODEL)), full((HYENA_WIDTH, D_MODEL)),
                  full((D_MODEL, D_MODEL)), full((1, D_MODEL)), full((1, D_MODEL))],
        out_specs=row(D_MODEL),
        out_shape=jax.ShapeDtypeStruct((rows, D_MODEL), f32),
        compiler_params=pltpu.CompilerParams(dimension_semantics=("arbitrary",),
                                             vmem_limit_bytes=VMEM_LIMIT),
        name="merge_out",
    )(x2d, ya, yh, wbr, wa, wh, wo, ln_g, ln_b)


def _layer(x, w_in, rel_bias, attn_sink, conv_w, conv_b, filt_w1, filt_b1, filt_w2, filt_b2,
           filt_w3, filt_b3, filt_w4, filt_freq, hyena_skip, w_branch_attn, w_branch_hyena,
           w_out, ln_g, ln_b):
    batch = x.shape[0]
    wdft_np, vdft_np = _dft_matrices()
    feat_np, decay_np = _filter_constants()
    wdft = jnp.asarray(wdft_np, dtype=bf16)
    vdft = jnp.asarray(vdft_np, dtype=bf16)
    row2d = lambda v: v.reshape(1, -1).astype(f32)

    w1p = jnp.pad(filt_w1.astype(f32), ((0, FEAT_PAD - FILTER_EMB), (0, 0)))
    kre, kim = _filter_spectrum(w1p, row2d(filt_b1), filt_w2.astype(f32), row2d(filt_b2),
                                filt_w3.astype(f32), row2d(filt_b3), row2d(filt_freq),
                                filt_w4.astype(f32), jnp.asarray(feat_np), jnp.asarray(decay_np), wdft)

    w_in_b = w_in.astype(bf16)
    w_att = jnp.concatenate([w_in_b[:, COL_Q:COL_K], w_in_b[:, COL_AGATE:COL_HY],
                             w_in_b[:, COL_K:COL_V]], axis=1)
    wvt = w_in_b[:, COL_V:COL_AGATE].T
    u_att, vt, xb = _attn_proj(x.reshape(batch * SEQ, D_MODEL), w_att, wvt)
    y_a = _attn_branch(u_att, vt, rel_bias.astype(f32), attn_sink.astype(f32),
                       jnp.asarray(_bucket_table()), batch)

    y_h = _hyena_branch(xb.reshape(batch, SEQ, D_MODEL), w_in_b, conv_w.astype(f32), row2d(conv_b),
                        row2d(hyena_skip), kre, kim, wdft, vdft)

    out = _merge_out(x.reshape(batch * SEQ, D_MODEL), y_a, y_h.reshape(batch * SEQ, HYENA_WIDTH),
                     w_in_b[:, COL_BR:], w_branch_attn.astype(bf16), w_branch_hyena.astype(bf16),
                     w_out.astype(bf16), row2d(ln_g), row2d(ln_b))
    return out.reshape(x.shape).astype(x.dtype)


def kernel(x, w_in, rel_bias, attn_sink, conv_w, conv_b, filt_w1, filt_b1, filt_w2, filt_b2,
           filt_w3, filt_b3, filt_w4, filt_freq, hyena_skip, w_branch_attn, w_branch_hyena,
           w_out, ln_g, ln_b):
    h = x
    for l in range(DEPTH):
        h = _layer(h, w_in[l], rel_bias, attn_sink[l], conv_w[l], conv_b[l],
                   filt_w1[l], filt_b1[l], filt_w2[l], filt_b2[l], filt_w3[l], filt_b3[l],
                   filt_w4[l], filt_freq[l], hyena_skip[l], w_branch_attn[l],
                   w_branch_hyena[l], w_out[l], ln_g[l], ln_b[l])
    return h
```

```python
import functools
import math

import numpy as np
import jax
import jax.numpy as jnp
from jax import lax
from jax.experimental import pallas as pl
from jax.experimental.pallas import tpu as pltpu

D_MODEL = 1024
SEQ = 2048
N_HEADS = 16
N_KV_HEADS = 4
HEAD_DIM = 64
GROUP = N_HEADS // N_KV_HEADS
ATTN_WIDTH = N_HEADS * HEAD_DIM
KV_WIDTH = N_KV_HEADS * HEAD_DIM
WINDOW = 128
BLOCK = 128
REL_BUCKETS = 32
REL_MAX_DIST = 128
NEG_INF = -1e30
HYENA_WIDTH = 1024
FILTER_EMB = 33
FILTER_HIDDEN = 64
DECAY_FAST = 0.3
DECAY_SLOW = 1.5
DECAY_TARGET = 1e-2
DEPTH = 1
DEEPNORM_ALPHA = (2 * DEPTH) ** 0.25
LN_EPS = 1e-5

COL_Q = 0
COL_K = ATTN_WIDTH
COL_V = COL_K + KV_WIDTH
COL_AGATE = COL_V + KV_WIDTH
COL_HY = COL_AGATE + ATTN_WIDTH
COL_HGATE = COL_HY + 3 * HYENA_WIDTH
COL_BR = COL_HGATE + HYENA_WIDTH
IN_COLS = COL_BR + 2 * D_MODEL

DFT_N = 2 * SEQ
RADIX = 4
SUB = SEQ // RADIX
CT = 256
LANE = 128
FEAT_PAD = 128

VMEM_LIMIT = 56 * 1024 * 1024

f32 = jnp.float32
bf16 = jnp.bfloat16


@functools.lru_cache(maxsize=None)
def _dft_matrices():
    f = np.arange(SUB, dtype=np.int64)[:, None]
    m = np.arange(SUB, dtype=np.int64)[None, :]
    w = np.empty((RADIX, 2 * SUB, SUB), np.float32)
    for r in range(RADIX):
        idx = ((2 * f + 1) * (RADIX * m + r)) % (2 * DFT_N)
        ang = np.pi * idx.astype(np.float64) / DFT_N
        w[r, :SUB] = np.cos(ang)
        w[r, SUB:] = -np.sin(ang)
    v = np.ascontiguousarray(np.transpose(w, (0, 2, 1))) * np.float32(2.0 / DFT_N)
    return w, v


@functools.lru_cache(maxsize=None)
def _filter_constants():
    bands = (FILTER_EMB - 1) // 2
    t = np.linspace(0.0, 1.0, SEQ, dtype=np.float32)[:, None]
    w = (2.0 * math.pi * np.arange(SEQ, dtype=np.float32)[:, None] / SEQ).astype(np.float32)
    fb = np.linspace(1e-4, bands - 1, bands, dtype=np.float32)[None, :]
    arg = (fb * w).astype(np.float64)
    feat = np.zeros((SEQ, FEAT_PAD), np.float32)
    feat[:, 0:1] = t
    feat[:, 1:1 + bands] = np.cos(arg)
    feat[:, 1 + bands:1 + 2 * bands] = -np.sin(arg)
    max_decay = math.log(DECAY_TARGET) / DECAY_FAST
    min_decay = math.log(DECAY_TARGET) / DECAY_SLOW
    deltas = np.linspace(min_decay, max_decay, HYENA_WIDTH, dtype=np.float32)
    decay = np.exp(-t.astype(np.float64) * np.abs(deltas.astype(np.float64))).astype(np.float32)
    return feat, decay


@functools.lru_cache(maxsize=None)
def _bucket_table():
    a = np.arange(BLOCK)[:, None]
    c = np.arange(3 * BLOCK)[None, :]
    rel = c - BLOCK - a
    half = REL_BUCKETS // 2
    max_exact = half // 2
    ret = (rel > 0).astype(np.int32) * half
    n = np.abs(rel)
    n_safe = np.maximum(n, 1).astype(np.float32)
    large = max_exact + (np.log(n_safe / max_exact) / math.log(REL_MAX_DIST / max_exact)
                         * (half - max_exact)).astype(np.int32)
    large = np.minimum(large, half - 1)
    bucket = (ret + np.where(n < max_exact, n, large)).astype(np.int32)
    return np.ascontiguousarray(np.where(np.abs(rel) <= WINDOW, bucket, -1).astype(np.int32).T)


def _store_lane_split(ref, val):
    for j in range(ref.shape[0]):
        ref[j] = val[:, j * LANE:(j + 1) * LANE]


def _load_lane_split(ref, rows=slice(None)):
    return jnp.concatenate([ref[j, rows, :] for j in range(ref.shape[0])], axis=1)


def _dft_forward(z_ref, w_ref):
    a = []
    for r in range(RADIX):
        zr = _load_lane_split(z_ref, pl.ds(r, SUB, stride=RADIX)).astype(bf16)
        ar = jnp.dot(w_ref[r], zr, preferred_element_type=f32)
        a.append((ar[:SUB], ar[SUB:]))
    s02 = (a[0][0] + a[2][0], a[0][1] + a[2][1])
    d02 = (a[0][0] - a[2][0], a[0][1] - a[2][1])
    s13 = (a[1][0] + a[3][0], a[1][1] + a[3][1])
    d13 = (a[1][0] - a[3][0], a[1][1] - a[3][1])
    x0 = (s02[0] + s13[0], s02[1] + s13[1])
    x2 = (s02[0] - s13[0], s02[1] - s13[1])
    x1 = (d02[0] + d13[1], d02[1] - d13[0])
    x3 = (d02[0] - d13[1], d02[1] + d13[0])
    return [x0, x1, x2, x3]


def _dft_inverse(p, v_ref, y_ref):
    s02 = (p[0][0] + p[2][0], p[0][1] + p[2][1])
    d02 = (p[0][0] - p[2][0], p[0][1] - p[2][1])
    s13 = (p[1][0] + p[3][0], p[1][1] + p[3][1])
    d13 = (p[1][0] - p[3][0], p[1][1] - p[3][1])
    q = [
        (s02[0] + s13[0], s02[1] + s13[1]),
        (d02[0] - d13[1], d02[1] + d13[0]),
        (s02[0] - s13[0], s02[1] - s13[1]),
        (d02[0] + d13[1], d02[1] - d13[0]),
    ]
    for r in range(RADIX):
        qr = jnp.concatenate([q[r][0], q[r][1]], axis=0).astype(bf16)
        yr = jnp.dot(v_ref[r], qr, preferred_element_type=f32)
        for j in range(y_ref.shape[0]):
            y_ref[j, pl.ds(r, SUB, stride=RADIX), :] = yr[:, j * LANE:(j + 1) * LANE]


def _filter_kernel(feat_ref, w1_ref, b1_ref, w2_ref, b2_ref, w3_ref, b3_ref, fr_ref,
                   w4f_ref, w4b_ref, decay_ref, wdft_ref, kre_ref, kim_ref, h3_ref, sig_ref):
    hi = lax.Precision.HIGHEST

    @pl.when(pl.program_id(0) == 0)
    def _():
        fr = fr_ref[...]
        h = jnp.sin(fr * (jnp.dot(feat_ref[...], w1_ref[...], precision=hi,
                                  preferred_element_type=f32) + b1_ref[...]))
        h = jnp.sin(fr * (jnp.dot(h, w2_ref[...], precision=hi, preferred_element_type=f32) + b2_ref[...]))
        h = jnp.sin(fr * (jnp.dot(h, w3_ref[...], precision=hi, preferred_element_type=f32) + b3_ref[...]))
        h3_ref[...] = h

    h3 = h3_ref[...]
    decay = decay_ref[...]
    hf = jnp.dot(h3, w4f_ref[...], precision=hi, preferred_element_type=f32) * decay
    hb = jnp.dot(h3, w4b_ref[...], precision=hi, preferred_element_type=f32) * decay
    row = lax.broadcasted_iota(jnp.int32, hb.shape, 0)
    hb = jnp.where(row == 0, 0.0, hb)
    _store_lane_split(sig_ref, hf + hb)
    xs = _dft_forward(sig_ref, wdft_ref)
    for k in range(RADIX):
        kre_ref[k * SUB:(k + 1) * SUB, :] = xs[k][0]
    _store_lane_split(sig_ref, hf - hb)
    xd = _dft_forward(sig_ref, wdft_ref)
    for k in range(RADIX):
        kim_ref[k * SUB:(k + 1) * SUB, :] = xd[k][1]


def _filter_spectrum(w1p, b1, w2, b2, w3, b3, fr, w4, feat, decay, wdft):
    nct = HYENA_WIDTH // CT
    full = lambda shape: pl.BlockSpec(shape, lambda c: (0,) * len(shape))
    return pl.pallas_call(
        _filter_kernel,
        grid=(nct,),
        in_specs=[
            full((SEQ, FEAT_PAD)), full((FEAT_PAD, FILTER_HIDDEN)), full((1, FILTER_HIDDEN)),
            full((FILTER_HIDDEN, FILTER_HIDDEN)), full((1, FILTER_HIDDEN)),
            full((FILTER_HIDDEN, FILTER_HIDDEN)), full((1, FILTER_HIDDEN)), full((1, FILTER_HIDDEN)),
            pl.BlockSpec((FILTER_HIDDEN, CT), lambda c: (0, c)),
            pl.BlockSpec((FILTER_HIDDEN, CT), lambda c: (0, nct + c)),
            pl.BlockSpec((SEQ, CT), lambda c: (0, c)),
            full((RADIX, 2 * SUB, SUB)),
        ],
        out_specs=[pl.BlockSpec((SEQ, CT), lambda c: (0, c)), pl.BlockSpec((SEQ, CT), lambda c: (0, c))],
        out_shape=[jax.ShapeDtypeStruct((SEQ, HYENA_WIDTH), f32)] * 2,
        scratch_shapes=[pltpu.VMEM((SEQ, FILTER_HIDDEN), f32), pltpu.VMEM((CT // LANE, SEQ, LANE), f32)],
        compiler_params=pltpu.CompilerParams(dimension_semantics=("arbitrary",),
                                             vmem_limit_bytes=VMEM_LIMIT),
        name="filter_spectrum",
    )(feat, w1p, b1, w2, b2, w3, b3, fr, w4, w4, decay, wdft)


CONV_ROWS = 512
HALF_ROWS = CONV_ROWS // 2
PIECE_ROWS = 64
PACK_ROWS = 16
SPEC_ROWS = PACK_ROWS


def _fold_rows(val, acc):
    piece = val[:PACK_ROWS]
    return piece if acc is None else acc + piece


def _exact_zero(fold):
    return jnp.minimum(jnp.abs(fold), 0.0)


def _anchor(w, zero):
    head = (w[:PACK_ROWS].astype(f32) + zero).astype(w.dtype)
    return jnp.concatenate([head, w[PACK_ROWS:]], axis=0)


def _fill_neighbours(ext_ref):
    row = lax.broadcasted_iota(jnp.int32, (SUB, CT), 0)
    last = ext_ref[SEQ - 1:SEQ + SUB - 1, :]
    ext_ref[0:SUB, :] = jnp.where(row == 0, 0.0, last)
    first = ext_ref[SUB + 1:2 * SUB + 1, :]
    ext_ref[SUB + SEQ:2 * SUB + SEQ, :] = jnp.where(row == SUB - 1, 0.0, first)


def _short_conv(ext_ref, i0, rows, cw_ref, cb):
    cw = cw_ref[...]
    return (cb + cw[0:1] * ext_ref[pl.ds(i0, rows), :] + cw[1:2] * ext_ref[pl.ds(SUB + i0, rows), :]
            + cw[2:3] * ext_ref[pl.ds(2 * SUB + i0, rows), :])


def _hyena_kernel(x_ref, wx0_ref, wx1_ref, wv_ref, wg_ref,
                  cw0_ref, cw1_ref, cwv_ref, cb0_ref, cb1_ref, cbv_ref, skip_ref,
                  kre_ref, kim_ref, wdft_ref, vdft_ref, o_ref,
                  e1_ref, ev_ref, e0_ref, gate_ref, z_ref, zb_ref, a_ref, q_ref):
    halves = CT // LANE

    def mm(a, b):
        return jnp.dot(a, b, preferred_element_type=f32)

    def conv_z(i0, rows):
        fold = None
        for c in range(i0, i0 + rows, PIECE_ROWS):
            z = (_short_conv(ev_ref, c, PIECE_ROWS, cwv_ref, cbv_ref[...])
                 * _short_conv(e1_ref, c, PIECE_ROWS, cw1_ref, cb1_ref[...]))
            z_ref[c:c + PIECE_ROWS, :] = z
            zb_ref[c:c + PIECE_ROWS, :] = z.astype(bf16)
            fold = _fold_rows(z, fold)
        return _exact_zero(fold)

    def gate_rows(i0, rows):
        fold = None
        for c in range(i0, i0 + rows, PIECE_ROWS):
            g = gate_ref[c:c + PIECE_ROWS, :]
            gated = _short_conv(e0_ref, c, PIECE_ROWS, cw0_ref, cb0_ref[...]) * (g / (1.0 + jnp.exp(-g)))
            gate_ref[c:c + PIECE_ROWS, :] = gated
            fold = _fold_rows(gated, fold)
        return _exact_zero(fold)

    def spectral(f_start, rows):
        fold = None
        for f0 in range(f_start, f_start + rows, SPEC_ROWS):
            def a_rows(r, base):
                return jnp.concatenate(
                    [a_ref[r * halves + j, base + f0:base + f0 + SPEC_ROWS, :] for j in range(halves)], axis=1)

            are = [a_rows(r, 0) for r in range(RADIX)]
            aim = [a_rows(r, SUB) for r in range(RADIX)]
            s02 = (are[0] + are[2], aim[0] + aim[2])
            d02 = (are[0] - are[2], aim[0] - aim[2])
            s13 = (are[1] + are[3], aim[1] + aim[3])
            d13 = (are[1] - are[3], aim[1] - aim[3])
            xs = [(s02[0] + s13[0], s02[1] + s13[1]),
                  (d02[0] + d13[1], d02[1] - d13[0]),
                  (s02[0] - s13[0], s02[1] - s13[1]),
                  (d02[0] - d13[1], d02[1] + d13[0])]
            p = []
            for k in range(RADIX):
                kre = kre_ref[k * SUB + f0:k * SUB + f0 + SPEC_ROWS, :]
                kim = kim_ref[k * SUB + f0:k * SUB + f0 + SPEC_ROWS, :]
                xr, xi = xs[k]
                p.append((xr * kre - xi * kim, xr * kim + xi * kre))
            s02 = (p[0][0] + p[2][0], p[0][1] + p[2][1])
            d02 = (p[0][0] - p[2][0], p[0][1] - p[2][1])
            s13 = (p[1][0] + p[3][0], p[1][1] + p[3][1])
            d13 = (p[1][0] - p[3][0], p[1][1] - p[3][1])
            qs = [(s02[0] + s13[0], s02[1] + s13[1]),
                  (d02[0] - d13[1], d02[1] + d13[0]),
                  (s02[0] - s13[0], s02[1] - s13[1]),
                  (d02[0] + d13[1], d02[1] - d13[0])]
            for r in range(RADIX):
                q_ref[r, f0:f0 + SPEC_ROWS, :] = qs[r][0].astype(bf16)
                q_ref[r, SUB + f0:SUB + f0 + SPEC_ROWS, :] = qs[r][1].astype(bf16)
                fold = _fold_rows(qs[r][0] + qs[r][1], fold)
        return _exact_zero(fold)

    def proj_rows(i0, dep):
        xj = x_ref[i0:i0 + HALF_ROWS, :]
        return mm(xj, _anchor(wx0_ref[...], dep)), mm(xj, _anchor(wg_ref[...], dep))

    def store_proj(i0, e0, g):
        e0_ref[SUB + i0:SUB + i0 + HALF_ROWS, :] = e0
        gate_ref[i0:i0 + HALF_ROWS, :] = g

    e1_ref[SUB:SUB + SEQ, :] = mm(x_ref[...], wx1_ref[...])
    ev_ref[SUB:SUB + SEQ, :] = mm(x_ref[...], wv_ref[...])
    _fill_neighbours(e1_ref)
    _fill_neighbours(ev_ref)

    n_pairs = SEQ // 2 // HALF_ROWS
    dep = jnp.zeros((PACK_ROWS, CT), f32)
    for k in range(n_pairs):
        store_proj(k * HALF_ROWS, *proj_rows(k * HALF_ROWS, dep))
        dep = conv_z(k * (SEQ // n_pairs), SEQ // n_pairs)

    for r in range(RADIX):
        ar = mm(wdft_ref[r], zb_ref[r * SUB:(r + 1) * SUB, :])
        for j in range(halves):
            a_ref[r * halves + j] = ar[:, j * LANE:(j + 1) * LANE]

    for k in range(n_pairs):
        i0 = SEQ // 2 + k * HALF_ROWS
        store_proj(i0, *proj_rows(i0, dep))
        dep = spectral(k * (SUB // n_pairs), SUB // n_pairs)
    _fill_neighbours(e0_ref)

    half_rows = SUB // 2
    for r in range(RADIX):
        blk = slice(r * SUB, (r + 1) * SUB)
        y = mm(vdft_ref[r], _anchor(q_ref[r], dep))
        dep = gate_rows(r * SUB, SUB)
        o = (y + z_ref[blk, :] * skip_ref[...]) * gate_ref[blk, :]
        for j in range(halves):
            for h in range(2):
                a_ref[j * 2 + h, pl.ds(r, half_rows, stride=RADIX), :] = (
                    o[h * half_rows:(h + 1) * half_rows, j * LANE:(j + 1) * LANE])
    o_ref[...] = jnp.concatenate(
        [jnp.concatenate([a_ref[j * 2 + h, 0:SEQ // 2, :] for h in range(2)], axis=0) for j in range(halves)],
        axis=1).astype(o_ref.dtype)


def _hyena_branch(xb, w_in_b, conv_w, conv_b, skip, kre, kim, wdft, vdft):
    batch = xb.shape[0]
    nct = HYENA_WIDTH // CT
    wcol = lambda base: pl.BlockSpec((D_MODEL, CT), lambda c, b: (0, base // CT + c))
    ccol = lambda rows, part: pl.BlockSpec((rows, CT), lambda c, b: (0, part * nct + c))
    tile = pl.BlockSpec((SEQ, CT), lambda c, b: (0, c))
    return pl.pallas_call(
        _hyena_kernel,
        grid=(nct, batch),
        in_specs=[
            pl.BlockSpec((None, SEQ, D_MODEL), lambda c, b: (b, 0, 0)),
            wcol(COL_HY), wcol(COL_HY + HYENA_WIDTH), wcol(COL_HY + 2 * HYENA_WIDTH), wcol(COL_HGATE),
            ccol(3, 0), ccol(3, 1), ccol(3, 2), ccol(1, 0), ccol(1, 1), ccol(1, 2),
            pl.BlockSpec((1, CT), lambda c, b: (0, c)),
            tile, tile,
            pl.BlockSpec((RADIX, 2 * SUB, SUB), lambda c, b: (0, 0, 0)),
            pl.BlockSpec((RADIX, SUB, 2 * SUB), lambda c, b: (0, 0, 0)),
        ],
        out_specs=pl.BlockSpec((None, SEQ, CT), lambda c, b: (b, 0, c)),
        out_shape=jax.ShapeDtypeStruct((batch, SEQ, HYENA_WIDTH), bf16),
        scratch_shapes=[pltpu.VMEM((SEQ + 2 * SUB, CT), f32), pltpu.VMEM((SEQ + 2 * SUB, CT), f32),
                        pltpu.VMEM((SEQ + 2 * SUB, CT), f32), pltpu.VMEM((SEQ, CT), f32),
                        pltpu.VMEM((SEQ, CT), f32), pltpu.VMEM((SEQ, CT), bf16),
                        pltpu.VMEM((RADIX * (CT // LANE), 2 * SUB, LANE), f32),
                        pltpu.VMEM((RADIX, 2 * SUB, CT), bf16)],
        compiler_params=pltpu.CompilerParams(dimension_semantics=("arbitrary", "arbitrary"),
                                             vmem_limit_bytes=VMEM_LIMIT),
        name="hyena_branch",
    )(xb, w_in_b, w_in_b, w_in_b, w_in_b, conv_w, conv_w, conv_w, conv_b, conv_b, conv_b,
      skip, kre, kim, wdft, vdft)


ATT_COLS = 2 * ATTN_WIDTH + KV_WIDTH
PROJ_TM = 1024
PROJ_TN = 256
LOG2E = math.log2(math.e)


def _attn_proj_kernel(x_ref, w_ref, wvt_ref, o_ref, vt_ref, xb_ref, xs_ref):
    for j in range(D_MODEL // LANE):
        xs_ref[j] = x_ref[:, j * LANE:(j + 1) * LANE]
    for r in range(RADIX):
        xb_ref[r] = jnp.concatenate(
            [xs_ref[j, pl.ds(r, PROJ_TM // RADIX, stride=RADIX), :] for j in range(D_MODEL // LANE)],
            axis=1).astype(bf16)
    x = x_ref[...].astype(bf16)
    q_scale = HEAD_DIM ** -0.5 * LOG2E
    for j in range(ATT_COLS // PROJ_TN):
        cols = slice(j * PROJ_TN, (j + 1) * PROJ_TN)
        acc = jnp.dot(x, w_ref[:, cols], preferred_element_type=f32)
        if (j + 1) * PROJ_TN <= ATTN_WIDTH:
            acc = acc * q_scale
        o_ref[:, cols] = acc.astype(o_ref.dtype)
    vt = lax.dot_general(wvt_ref[...], x, (((1,), (1,)), ((), ())), preferred_element_type=f32)
    for j in range(PROJ_TM // BLOCK):
        vt_ref[j] = vt[:, j * BLOCK:(j + 1) * BLOCK].astype(vt_ref.dtype)


def _attn_proj(x2d, w_att, wvt):
    rows = x2d.shape[0]
    tiles_per_seq = SEQ // PROJ_TM
    return pl.pallas_call(
        _attn_proj_kernel,
        grid=(rows // PROJ_TM,),
        in_specs=[pl.BlockSpec((PROJ_TM, D_MODEL), lambda i: (i, 0)),
                  pl.BlockSpec((D_MODEL, ATT_COLS), lambda i: (0, 0)),
                  pl.BlockSpec((KV_WIDTH, D_MODEL), lambda i: (0, 0))],
        out_specs=[pl.BlockSpec((PROJ_TM, ATT_COLS), lambda i: (i, 0)),
                   pl.BlockSpec((PROJ_TM // BLOCK, KV_WIDTH, BLOCK), lambda i: (i, 0, 0)),
                   pl.BlockSpec((None, RADIX, PROJ_TM // RADIX, D_MODEL),
                                lambda i: (i // tiles_per_seq, 0, i % tiles_per_seq, 0))],
        out_shape=[jax.ShapeDtypeStruct((rows, ATT_COLS), bf16),
                   jax.ShapeDtypeStruct((rows // BLOCK, KV_WIDTH, BLOCK), bf16),
                   jax.ShapeDtypeStruct((rows // SEQ, RADIX, SUB, D_MODEL), bf16)],
        scratch_shapes=[pltpu.VMEM((D_MODEL // LANE, PROJ_TM, LANE), f32)],
        compiler_params=pltpu.CompilerParams(dimension_semantics=("arbitrary",),
                                             vmem_limit_bytes=VMEM_LIMIT),
        name="attn_proj",
    )(x2d, w_att, wvt)


N_QBLOCKS = SEQ // BLOCK
BAND = 3 * BLOCK
GQ = GROUP * BLOCK


def _attn_kernel(relb_ref, sink_ref, bucket_ref, q_ref, g_ref, k_ref, vt_ref, o_ref, bias_ref):
    @pl.when(pl.program_id(0) == 0)
    def _():
        bucket = bucket_ref[...]
        neg = jnp.full((BLOCK, GQ), NEG_INF, f32)
        for h in range(N_KV_HEADS):
            cols = []
            for g in range(GROUP):
                acc = jnp.full(bucket.shape, NEG_INF, f32)
                for kb in range(REL_BUCKETS):
                    acc = jnp.where(bucket == kb, relb_ref[kb, h * GROUP + g] * LOG2E, acc)
                cols.append(acc)
            tbl = jnp.concatenate(cols, axis=1)
            bias_ref[0, h] = tbl
            bias_ref[1, h] = jnp.concatenate([neg, tbl[BLOCK:]], axis=0)
            bias_ref[2, h] = jnp.concatenate([tbl[:2 * BLOCK], neg], axis=0)

    sink_rows = [
        jnp.concatenate([jnp.full((1, BLOCK), sink_ref[h * GROUP + g] * LOG2E, f32) for g in range(GROUP)], axis=1)
        for h in range(N_KV_HEADS)]

    def qblock(n, carry):
        pj = jnp.maximum(n - 1, 0)
        nj = jnp.minimum(n + 1, N_QBLOCKS - 1)
        start = pl.multiple_of(n * BLOCK, BLOCK)
        prev = pl.multiple_of(pj * BLOCK, BLOCK)
        nxt = pl.multiple_of(nj * BLOCK, BLOCK)
        variant = jnp.where(n == 0, 1, jnp.where(n == N_QBLOCKS - 1, 2, 0))
        rows = pl.ds(start, BLOCK)

        def scores(h):
            hd = slice(h * HEAD_DIM, (h + 1) * HEAD_DIM)
            kband = jnp.concatenate([k_ref[pl.ds(prev, BLOCK), hd], k_ref[rows, hd],
                                     k_ref[pl.ds(nxt, BLOCK), hd]], axis=0)
            qs = jnp.concatenate(
                [q_ref[rows, (h * GROUP + g) * HEAD_DIM:(h * GROUP + g + 1) * HEAD_DIM] for g in range(GROUP)],
                axis=0)
            return lax.dot_general(kband, qs, (((1,), (1,)), ((), ())), preferred_element_type=f32)

        st_next = scores(0)
        for h in range(N_KV_HEADS):
            hd = slice(h * HEAD_DIM, (h + 1) * HEAD_DIM)
            st = st_next + bias_ref[variant, h]
            if h + 1 < N_KV_HEADS:
                st_next = scores(h + 1)
            vtband = jnp.concatenate([vt_ref[pj, hd, :], vt_ref[n, hd, :], vt_ref[nj, hd, :]], axis=1)
            ones = jnp.ones((16, BAND), f32)
            if h + 1 < N_KV_HEADS:
                ones = ones + jnp.minimum(jnp.abs(st_next[:16, :BAND]), 0.0)
            vtband = jnp.concatenate([vtband, ones.astype(bf16)], axis=0)
            m = jnp.maximum(jnp.max(st, axis=0, keepdims=True), sink_rows[h])
            e = jnp.exp2(st - m)
            ot = jnp.dot(vtband, e.astype(bf16), preferred_element_type=f32)
            den = ot[HEAD_DIM:HEAD_DIM + 1] + jnp.exp2(sink_rows[h] - m)
            ot = ot[:HEAD_DIM] * (1.0 / den)
            for p in range(GROUP // 2):
                pair = jnp.concatenate([ot[:, (2 * p) * BLOCK:(2 * p + 1) * BLOCK],
                                        ot[:, (2 * p + 1) * BLOCK:(2 * p + 2) * BLOCK]], axis=0)
                c0 = (h * GROUP + 2 * p) * HEAD_DIM
                gate = g_ref[rows, c0:c0 + 2 * HEAD_DIM].astype(f32)
                o_ref[rows, c0:c0 + 2 * HEAD_DIM] = (pair.T * (gate / (1.0 + jnp.exp(-gate)))).astype(o_ref.dtype)
        return carry

    lax.fori_loop(0, N_QBLOCKS, qblock, 0, unroll=2)


def _attn_branch(u_att, vt, rel_bias, sink, bucket_t, batch):
    return pl.pallas_call(
        _attn_kernel,
        grid=(batch,),
        in_specs=[
            pl.BlockSpec(memory_space=pltpu.SMEM),
            pl.BlockSpec(memory_space=pltpu.SMEM),
            pl.BlockSpec((BAND, BLOCK), lambda b: (0, 0)),
            pl.BlockSpec((SEQ, ATTN_WIDTH), lambda b: (b, 0)),
            pl.BlockSpec((SEQ, ATTN_WIDTH), lambda b: (b, 1)),
            pl.BlockSpec((SEQ, KV_WIDTH), lambda b: (b, 2 * ATTN_WIDTH // KV_WIDTH)),
            pl.BlockSpec((N_QBLOCKS, KV_WIDTH, BLOCK), lambda b: (b, 0, 0)),
        ],
        out_specs=pl.BlockSpec((SEQ, ATTN_WIDTH), lambda b: (b, 0)),
        out_shape=jax.ShapeDtypeStruct((batch * SEQ, ATTN_WIDTH), bf16),
        scratch_shapes=[pltpu.VMEM((3, N_KV_HEADS, BAND, GQ), f32)],
        compiler_params=pltpu.CompilerParams(dimension_semantics=("arbitrary",),
                                             vmem_limit_bytes=VMEM_LIMIT),
        name="attn_branch",
    )(rel_bias, sink, bucket_t, u_att, u_att, u_att, vt)


OUT_TM = 512


def _merge_kernel(x_ref, ya_ref, yh_ref, wbr_ref, wa_ref, wh_ref, wo_ref, g_ref, b_ref, o_ref):
    x = x_ref[...]
    br = jnp.dot(x.astype(bf16), wbr_ref[...], preferred_element_type=f32)
    gates = 1.0 / (1.0 + jnp.exp(-br))
    pa = jnp.dot(ya_ref[...], wa_ref[...], preferred_element_type=f32)
    ph = jnp.dot(yh_ref[...], wh_ref[...], preferred_element_type=f32)
    merged = gates[:, :D_MODEL] * pa + gates[:, D_MODEL:] * ph
    out = jnp.dot(merged.astype(bf16), wo_ref[...], preferred_element_type=f32)
    r = DEEPNORM_ALPHA * x + out
    mu = jnp.mean(r, axis=-1, keepdims=True)
    d = r - mu
    var = jnp.mean(d * d, axis=-1, keepdims=True)
    o_ref[...] = d * lax.rsqrt(var + LN_EPS) * g_ref[...] + b_ref[...]


def _merge_out(x2d, ya, yh, wbr, wa, wh, wo, ln_g, ln_b):
    rows = x2d.shape[0]
    row = lambda width: pl.BlockSpec((OUT_TM, width), lambda i: (i, 0))
    full = lambda shape: pl.BlockSpec(shape, lambda i: (0, 0))
    return pl.pallas_call(
        _merge_kernel,
        grid=(rows // OUT_TM,),
        in_specs=[row(D_MODEL), row(ATTN_WIDTH), row(HYENA_WIDTH),
                  full((D_MODEL, 2 * D_MODEL)), full((ATTN_WIDTH, D_MODEL)), full((HYENA_WIDTH, D_MODEL)),
                  full((D_MODEL, D_MODEL)), full((1, D_MODEL)), full((1, D_MODEL))],
        out_specs=row(D_MODEL),
        out_shape=jax.ShapeDtypeStruct((rows, D_MODEL), f32),
        compiler_params=pltpu.CompilerParams(dimension_semantics=("arbitrary",),
                                             vmem_limit_bytes=VMEM_LIMIT),
        name="merge_out",
    )(x2d, ya, yh, wbr, wa, wh, wo, ln_g, ln_b)


def _layer(x, w_in, rel_bias, attn_sink, conv_w, conv_b, filt_w1, filt_b1, filt_w2, filt_b2,
           filt_w3, filt_b3, filt_w4, filt_freq, hyena_skip, w_branch_attn, w_branch_hyena,
           w_out, ln_g, ln_b):
    batch = x.shape[0]
    wdft_np, vdft_np = _dft_matrices()
    feat_np, decay_np = _filter_constants()
    wdft = jnp.asarray(wdft_np, dtype=bf16)
    vdft = jnp.asarray(vdft_np, dtype=bf16)
    row2d = lambda v: v.reshape(1, -1).astype(f32)

    w1p = jnp.pad(filt_w1.astype(f32), ((0, FEAT_PAD - FILTER_EMB), (0, 0)))
    kre, kim = _filter_spectrum(w1p, row2d(filt_b1), filt_w2.astype(f32), row2d(filt_b2),
                                filt_w3.astype(f32), row2d(filt_b3), row2d(filt_freq),
                                filt_w4.astype(f32), jnp.asarray(feat_np), jnp.asarray(decay_np), wdft)

    w_in_b = w_in.astype(bf16)
    w_att = jnp.concatenate([w_in_b[:, COL_Q:COL_K], w_in_b[:, COL_AGATE:COL_HY],
                             w_in_b[:, COL_K:COL_V]], axis=1)
    wvt = w_in_b[:, COL_V:COL_AGATE].T
    u_att, vt, xb = _attn_proj(x.reshape(batch * SEQ, D_MODEL), w_att, wvt)
    y_a = _attn_branch(u_att, vt, rel_bias.astype(f32), attn_sink.astype(f32),
                       jnp.asarray(_bucket_table()), batch)

    y_h = _hyena_branch(xb.reshape(batch, SEQ, D_MODEL), w_in_b, conv_w.astype(f32), row2d(conv_b),
                        row2d(hyena_skip), kre, kim, wdft, vdft)

    out = _merge_out(x.reshape(batch * SEQ, D_MODEL), y_a, y_h.reshape(batch * SEQ, HYENA_WIDTH),
                     w_in_b[:, COL_BR:], w_branch_attn.astype(bf16), w_branch_hyena.astype(bf16),
                     w_out.astype(bf16), row2d(ln_g), row2d(ln_b))
    return out.reshape(x.shape).astype(x.dtype)


def kernel(x, w_in, rel_bias, attn_sink, conv_w, conv_b, filt_w1, filt_b1, filt_w2, filt_b2,
           filt_w3, filt_b3, filt_w4, filt_freq, hyena_skip, w_branch_attn, w_branch_hyena,
           w_out, ln_g, ln_b):
    h = x
    for l in range(DEPTH):
        h = _layer(h, w_in[l], rel_bias, attn_sink[l], conv_w[l], conv_b[l],
                   filt_w1[l], filt_b1[l], filt_w2[l], filt_b2[l], filt_w3[l], filt_b3[l],
                   filt_w4[l], filt_freq[l], hyena_skip[l], w_branch_attn[l],
                   w_branch_hyena[l], w_out[l], ln_g[l], ln_b[l])
    return h
```

```python
import functools
import math

import numpy as np
import jax
import jax.numpy as jnp
from jax import lax
from jax.experimental import pallas as pl
from jax.experimental.pallas import tpu as pltpu

D_MODEL = 1024
SEQ = 2048
N_HEADS = 16
N_KV_HEADS = 4
HEAD_DIM = 64
GROUP = N_HEADS // N_KV_HEADS
ATTN_WIDTH = N_HEADS * HEAD_DIM
KV_WIDTH = N_KV_HEADS * HEAD_DIM
WINDOW = 128
BLOCK = 128
REL_BUCKETS = 32
REL_MAX_DIST = 128
NEG_INF = -1e30
HYENA_WIDTH = 1024
FILTER_EMB = 33
FILTER_HIDDEN = 64
DECAY_FAST = 0.3
DECAY_SLOW = 1.5
DECAY_TARGET = 1e-2
DEPTH = 1
DEEPNORM_ALPHA = (2 * DEPTH) ** 0.25
LN_EPS = 1e-5

COL_Q = 0
COL_K = ATTN_WIDTH
COL_V = COL_K + KV_WIDTH
COL_AGATE = COL_V + KV_WIDTH
COL_HY = COL_AGATE + ATTN_WIDTH
COL_HGATE = COL_HY + 3 * HYENA_WIDTH
COL_BR = COL_HGATE + HYENA_WIDTH
IN_COLS = COL_BR + 2 * D_MODEL

DFT_N = 2 * SEQ
RADIX = 4
SUB = SEQ // RADIX
CT = 256
LANE = 128
FEAT_PAD = 128

VMEM_LIMIT = 56 * 1024 * 1024

f32 = jnp.float32
bf16 = jnp.bfloat16


@functools.lru_cache(maxsize=None)
def _dft_matrices():
    f = np.arange(SUB, dtype=np.int64)[:, None]
    m = np.arange(SUB, dtype=np.int64)[None, :]
    w = np.empty((RADIX, 2 * SUB, SUB), np.float32)
    for r in range(RADIX):
        idx = ((2 * f + 1) * (RADIX * m + r)) % (2 * DFT_N)
        ang = np.pi * idx.astype(np.float64) / DFT_N
        w[r, :SUB] = np.cos(ang)
        w[r, SUB:] = -np.sin(ang)
    v = np.ascontiguousarray(np.transpose(w, (0, 2, 1))) * np.float32(2.0 / DFT_N)
    return w, v


@functools.lru_cache(maxsize=None)
def _filter_constants():
    bands = (FILTER_EMB - 1) // 2
    t = np.linspace(0.0, 1.0, SEQ, dtype=np.float32)[:, None]
    w = (2.0 * math.pi * np.arange(SEQ, dtype=np.float32)[:, None] / SEQ).astype(np.float32)
    fb = np.linspace(1e-4, bands - 1, bands, dtype=np.float32)[None, :]
    arg = (fb * w).astype(np.float64)
    feat = np.zeros((SEQ, FEAT_PAD), np.float32)
    feat[:, 0:1] = t
    feat[:, 1:1 + bands] = np.cos(arg)
    feat[:, 1 + bands:1 + 2 * bands] = -np.sin(arg)
    max_decay = math.log(DECAY_TARGET) / DECAY_FAST
    min_decay = math.log(DECAY_TARGET) / DECAY_SLOW
    deltas = np.linspace(min_decay, max_decay, HYENA_WIDTH, dtype=np.float32)
    decay = np.exp(-t.astype(np.float64) * np.abs(deltas.astype(np.float64))).astype(np.float32)
    return feat, decay


@functools.lru_cache(maxsize=None)
def _bucket_table():
    a = np.arange(BLOCK)[:, None]
    c = np.arange(3 * BLOCK)[None, :]
    rel = c - BLOCK - a
    half = REL_BUCKETS // 2
    max_exact = half // 2
    ret = (rel > 0).astype(np.int32) * half
    n = np.abs(rel)
    n_safe = np.maximum(n, 1).astype(np.float32)
    large = max_exact + (np.log(n_safe / max_exact) / math.log(REL_MAX_DIST / max_exact)
                         * (half - max_exact)).astype(np.int32)
    large = np.minimum(large, half - 1)
    bucket = (ret + np.where(n < max_exact, n, large)).astype(np.int32)
    return np.ascontiguousarray(np.where(np.abs(rel) <= WINDOW, bucket, -1).astype(np.int32).T)


def _store_lane_split(ref, val):
    for j in range(ref.shape[0]):
        ref[j] = val[:, j * LANE:(j + 1) * LANE]


def _load_lane_split(ref, rows=slice(None)):
    return jnp.concatenate([ref[j, rows, :] for j in range(ref.shape[0])], axis=1)


def _dft_forward(z_ref, w_ref):
    a = []
    for r in range(RADIX):
        zr = _load_lane_split(z_ref, pl.ds(r, SUB, stride=RADIX)).astype(bf16)
        ar = jnp.dot(w_ref[r], zr, preferred_element_type=f32)
        a.append((ar[:SUB], ar[SUB:]))
    s02 = (a[0][0] + a[2][0], a[0][1] + a[2][1])
    d02 = (a[0][0] - a[2][0], a[0][1] - a[2][1])
    s13 = (a[1][0] + a[3][0], a[1][1] + a[3][1])
    d13 = (a[1][0] - a[3][0], a[1][1] - a[3][1])
    x0 = (s02[0] + s13[0], s02[1] + s13[1])
    x2 = (s02[0] - s13[0], s02[1] - s13[1])
    x1 = (d02[0] + d13[1], d02[1] - d13[0])
    x3 = (d02[0] - d13[1], d02[1] + d13[0])
    return [x0, x1, x2, x3]


def _dft_inverse(p, v_ref, y_ref):
    s02 = (p[0][0] + p[2][0], p[0][1] + p[2][1])
    d02 = (p[0][0] - p[2][0], p[0][1] - p[2][1])
    s13 = (p[1][0] + p[3][0], p[1][1] + p[3][1])
    d13 = (p[1][0] - p[3][0], p[1][1] - p[3][1])
    q = [
        (s02[0] + s13[0], s02[1] + s13[1]),
        (d02[0] - d13[1], d02[1] + d13[0]),
        (s02[0] - s13[0], s02[1] - s13[1]),
        (d02[0] + d13[1], d02[1] - d13[0]),
    ]
    for r in range(RADIX):
        qr = jnp.concatenate([q[r][0], q[r][1]], axis=0).astype(bf16)
        yr = jnp.dot(v_ref[r], qr, preferred_element_type=f32)
        for j in range(y_ref.shape[0]):
            y_ref[j, pl.ds(r, SUB, stride=RADIX), :] = yr[:, j * LANE:(j + 1) * LANE]


def _filter_kernel(feat_ref, w1_ref, b1_ref, w2_ref, b2_ref, w3_ref, b3_ref, fr_ref,
                   w4f_ref, w4b_ref, decay_ref, wdft_ref, kre_ref, kim_ref, h3_ref, sig_ref):
    hi = lax.Precision.HIGHEST

    @pl.when(pl.program_id(0) == 0)
    def _():
        fr = fr_ref[...]
        h = jnp.sin(fr * (jnp.dot(feat_ref[...], w1_ref[...], precision=hi,
                                  preferred_element_type=f32) + b1_ref[...]))
        h = jnp.sin(fr * (jnp.dot(h, w2_ref[...], precision=hi, preferred_element_type=f32) + b2_ref[...]))
        h = jnp.sin(fr * (jnp.dot(h, w3_ref[...], precision=hi, preferred_element_type=f32) + b3_ref[...]))
        h3_ref[...] = h

    h3 = h3_ref[...]
    decay = decay_ref[...]
    hf = jnp.dot(h3, w4f_ref[...], precision=hi, preferred_element_type=f32) * decay
    hb = jnp.dot(h3, w4b_ref[...], precision=hi, preferred_element_type=f32) * decay
    row = lax.broadcasted_iota(jnp.int32, hb.shape, 0)
    hb = jnp.where(row == 0, 0.0, hb)
    _store_lane_split(sig_ref, hf + hb)
    xs = _dft_forward(sig_ref, wdft_ref)
    for k in range(RADIX):
        kre_ref[k * SUB:(k + 1) * SUB, :] = xs[k][0]
    _store_lane_split(sig_ref, hf - hb)
    xd = _dft_forward(sig_ref, wdft_ref)
    for k in range(RADIX):
        kim_ref[k * SUB:(k + 1) * SUB, :] = xd[k][1]


def _filter_spectrum(w1p, b1, w2, b2, w3, b3, fr, w4, feat, decay, wdft):
    nct = HYENA_WIDTH // CT
    full = lambda shape: pl.BlockSpec(shape, lambda c: (0,) * len(shape))
    return pl.pallas_call(
        _filter_kernel,
        grid=(nct,),
        in_specs=[
            full((SEQ, FEAT_PAD)), full((FEAT_PAD, FILTER_HIDDEN)), full((1, FILTER_HIDDEN)),
            full((FILTER_HIDDEN, FILTER_HIDDEN)), full((1, FILTER_HIDDEN)),
            full((FILTER_HIDDEN, FILTER_HIDDEN)), full((1, FILTER_HIDDEN)), full((1, FILTER_HIDDEN)),
            pl.BlockSpec((FILTER_HIDDEN, CT), lambda c: (0, c)),
            pl.BlockSpec((FILTER_HIDDEN, CT), lambda c: (0, nct + c)),
            pl.BlockSpec((SEQ, CT), lambda c: (0, c)),
            full((RADIX, 2 * SUB, SUB)),
        ],
        out_specs=[pl.BlockSpec((SEQ, CT), lambda c: (0, c)), pl.BlockSpec((SEQ, CT), lambda c: (0, c))],
        out_shape=[jax.ShapeDtypeStruct((SEQ, HYENA_WIDTH), f32)] * 2,
        scratch_shapes=[pltpu.VMEM((SEQ, FILTER_HIDDEN), f32), pltpu.VMEM((CT // LANE, SEQ, LANE), f32)],
        compiler_params=pltpu.CompilerParams(dimension_semantics=("arbitrary",),
                                             vmem_limit_bytes=VMEM_LIMIT),
        name="filter_spectrum",
    )(feat, w1p, b1, w2, b2, w3, b3, fr, w4, w4, decay, wdft)


CONV_ROWS = 512
HALF_ROWS = CONV_ROWS // 2
PIECE_ROWS = 64
PACK_ROWS = 16
SPEC_ROWS = PACK_ROWS


def _fold_rows(val, acc):
    piece = val[:PACK_ROWS]
    return piece if acc is None else acc + piece


def _exact_zero(fold):
    return jnp.minimum(jnp.abs(fold), 0.0)


def _anchor(w, zero):
    head = (w[:PACK_ROWS].astype(f32) + zero).astype(w.dtype)
    return jnp.concatenate([head, w[PACK_ROWS:]], axis=0)


def _fill_neighbours(ext_ref):
    row = lax.broadcasted_iota(jnp.int32, (SUB, CT), 0)
    last = ext_ref[SEQ - 1:SEQ + SUB - 1, :]
    ext_ref[0:SUB, :] = jnp.where(row == 0, 0.0, last)
    first = ext_ref[SUB + 1:2 * SUB + 1, :]
    ext_ref[SUB + SEQ:2 * SUB + SEQ, :] = jnp.where(row == SUB - 1, 0.0, first)


def _short_conv(ext_ref, i0, rows, cw_ref, cb):
    cw = cw_ref[...]
    return (cb + cw[0:1] * ext_ref[pl.ds(i0, rows), :] + cw[1:2] * ext_ref[pl.ds(SUB + i0, rows), :]
            + cw[2:3] * ext_ref[pl.ds(2 * SUB + i0, rows), :])


def _hyena_kernel(x_ref, wx0_ref, wx1_ref, wv_ref, wg_ref,
                  cw0_ref, cw1_ref, cwv_ref, cb0_ref, cb1_ref, cbv_ref, skip_ref,
                  kre_ref, kim_ref, wdft_ref, vdft_ref, o_ref,
                  e1_ref, ev_ref, e0_ref, gate_ref, z_ref, zb_ref, a_ref, q_ref):
    halves = CT // LANE

    def mm(a, b):
        return jnp.dot(a, b, preferred_element_type=f32)

    def conv_z(i0, rows):
        fold = None
        for c in range(i0, i0 + rows, PIECE_ROWS):
            z = (_short_conv(ev_ref, c, PIECE_ROWS, cwv_ref, cbv_ref[...])
                 * _short_conv(e1_ref, c, PIECE_ROWS, cw1_ref, cb1_ref[...]))
            z_ref[c:c + PIECE_ROWS, :] = z
            zb_ref[c:c + PIECE_ROWS, :] = z.astype(bf16)
            fold = _fold_rows(z, fold)
        return _exact_zero(fold)

    def gate_rows(i0, rows):
        fold = None
        for c in range(i0, i0 + rows, PIECE_ROWS):
            g = gate_ref[c:c + PIECE_ROWS, :]
            gated = _short_conv(e0_ref, c, PIECE_ROWS, cw0_ref, cb0_ref[...]) * (g / (1.0 + jnp.exp(-g)))
            gate_ref[c:c + PIECE_ROWS, :] = gated
            fold = _fold_rows(gated, fold)
        return _exact_zero(fold)

    def spectral(f_start, rows):
        fold = None
        for f0 in range(f_start, f_start + rows, SPEC_ROWS):
            def a_rows(r, base):
                return jnp.concatenate(
                    [a_ref[r * halves + j, base + f0:base + f0 + SPEC_ROWS, :] for j in range(halves)], axis=1)

            are = [a_rows(r, 0) for r in range(RADIX)]
            aim = [a_rows(r, SUB) for r in range(RADIX)]
            s02 = (are[0] + are[2], aim[0] + aim[2])
            d02 = (are[0] - are[2], aim[0] - aim[2])
            s13 = (are[1] + are[3], aim[1] + aim[3])
            d13 = (are[1] - are[3], aim[1] - aim[3])
            xs = [(s02[0] + s13[0], s02[1] + s13[1]),
                  (d02[0] + d13[1], d02[1] - d13[0]),
                  (s02[0] - s13[0], s02[1] - s13[1]),
                  (d02[0] - d13[1], d02[1] + d13[0])]
            p = []
            for k in range(RADIX):
                kre = kre_ref[k * SUB + f0:k * SUB + f0 + SPEC_ROWS, :]
                kim = kim_ref[k * SUB + f0:k * SUB + f0 + SPEC_ROWS, :]
                xr, xi = xs[k]
                p.append((xr * kre - xi * kim, xr * kim + xi * kre))
            s02 = (p[0][0] + p[2][0], p[0][1] + p[2][1])
            d02 = (p[0][0] - p[2][0], p[0][1] - p[2][1])
            s13 = (p[1][0] + p[3][0], p[1][1] + p[3][1])
            d13 = (p[1][0] - p[3][0], p[1][1] - p[3][1])
            qs = [(s02[0] + s13[0], s02[1] + s13[1]),
                  (d02[0] - d13[1], d02[1] + d13[0]),
                  (s02[0] - s13[0], s02[1] - s13[1]),
                  (d02[0] + d13[1], d02[1] - d13[0])]
            for r in range(RADIX):
                q_ref[r, f0:f0 + SPEC_ROWS, :] = qs[r][0].astype(bf16)
                q_ref[r, SUB + f0:SUB + f0 + SPEC_ROWS, :] = qs[r][1].astype(bf16)
                fold = _fold_rows(qs[r][0] + qs[r][1], fold)
        return _exact_zero(fold)

    def proj_rows(i0, dep):
        xj = x_ref[i0:i0 + HALF_ROWS, :]
        return mm(xj, _anchor(wx0_ref[...], dep)), mm(xj, _anchor(wg_ref[...], dep))

    def store_proj(i0, e0, g):
        e0_ref[SUB + i0:SUB + i0 + HALF_ROWS, :] = e0
        gate_ref[i0:i0 + HALF_ROWS, :] = g

    e1_ref[SUB:SUB + SEQ, :] = mm(x_ref[...], wx1_ref[...])
    ev_ref[SUB:SUB + SEQ, :] = mm(x_ref[...], wv_ref[...])
    _fill_neighbours(e1_ref)
    _fill_neighbours(ev_ref)

    n_pairs = SEQ // 2 // HALF_ROWS
    dep = jnp.zeros((PACK_ROWS, CT), f32)
    for k in range(n_pairs):
        store_proj(k * HALF_ROWS, *proj_rows(k * HALF_ROWS, dep))
        dep = conv_z(k * (SEQ // n_pairs), SEQ // n_pairs)

    for r in range(RADIX):
        ar = mm(wdft_ref[r], zb_ref[r * SUB:(r + 1) * SUB, :])
        for j in range(halves):
            a_ref[r * halves + j] = ar[:, j * LANE:(j + 1) * LANE]

    for k in range(n_pairs):
        i0 = SEQ // 2 + k * HALF_ROWS
        store_proj(i0, *proj_rows(i0, dep))
        dep = spectral(k * (SUB // n_pairs), SUB // n_pairs)
    _fill_neighbours(e0_ref)

    half_rows = SUB // 2
    for r in range(RADIX):
        blk = slice(r * SUB, (r + 1) * SUB)
        y = mm(vdft_ref[r], _anchor(q_ref[r], dep))
        dep = gate_rows(r * SUB, SUB)
        o = (y + z_ref[blk, :] * skip_ref[...]) * gate_ref[blk, :]
        for j in range(halves):
            for h in range(2):
                a_ref[j * 2 + h, pl.ds(r, half_rows, stride=RADIX), :] = (
                    o[h * half_rows:(h + 1) * half_rows, j * LANE:(j + 1) * LANE])
    o_ref[...] = jnp.concatenate(
        [jnp.concatenate([a_ref[j * 2 + h, 0:SEQ // 2, :] for h in range(2)], axis=0) for j in range(halves)],
        axis=1).astype(o_ref.dtype)


def _hyena_branch(xb, w_in_b, conv_w, conv_b, skip, kre, kim, wdft, vdft):
    batch = xb.shape[0]
    nct = HYENA_WIDTH // CT
    wcol = lambda base: pl.BlockSpec((D_MODEL, CT), lambda c, b: (0, base // CT + c))
    ccol = lambda rows, part: pl.BlockSpec((rows, CT), lambda c, b: (0, part * nct + c))
    tile = pl.BlockSpec((SEQ, CT), lambda c, b: (0, c))
    return pl.pallas_call(
        _hyena_kernel,
        grid=(nct, batch),
        in_specs=[
            pl.BlockSpec((None, SEQ, D_MODEL), lambda c, b: (b, 0, 0)),
            wcol(COL_HY), wcol(COL_HY + HYENA_WIDTH), wcol(COL_HY + 2 * HYENA_WIDTH), wcol(COL_HGATE),
            ccol(3, 0), ccol(3, 1), ccol(3, 2), ccol(1, 0), ccol(1, 1), ccol(1, 2),
            pl.BlockSpec((1, CT), lambda c, b: (0, c)),
            tile, tile,
            pl.BlockSpec((RADIX, 2 * SUB, SUB), lambda c, b: (0, 0, 0)),
            pl.BlockSpec((RADIX, SUB, 2 * SUB), lambda c, b: (0, 0, 0)),
        ],
        out_specs=pl.BlockSpec((None, SEQ, CT), lambda c, b: (b, 0, c)),
        out_shape=jax.ShapeDtypeStruct((batch, SEQ, HYENA_WIDTH), bf16),
        scratch_shapes=[pltpu.VMEM((SEQ + 2 * SUB, CT), f32), pltpu.VMEM((SEQ + 2 * SUB, CT), f32),
                        pltpu.VMEM((SEQ + 2 * SUB, CT), f32), pltpu.VMEM((SEQ, CT), f32),
                        pltpu.VMEM((SEQ, CT), f32), pltpu.VMEM((SEQ, CT), bf16),
                        pltpu.VMEM((RADIX * (CT // LANE), 2 * SUB, LANE), f32),
                        pltpu.VMEM((RADIX, 2 * SUB, CT), bf16)],
        compiler_params=pltpu.CompilerParams(dimension_semantics=("arbitrary", "arbitrary"),
                                             vmem_limit_bytes=VMEM_LIMIT),
        name="hyena_branch",
    )(xb, w_in_b, w_in_b, w_in_b, w_in_b, conv_w, conv_w, conv_w, conv_b, conv_b, conv_b,
      skip, kre, kim, wdft, vdft)


ATT_COLS = 2 * ATTN_WIDTH + KV_WIDTH
PROJ_TM = 1024
PROJ_TN = 256
LOG2E = math.log2(math.e)


def _attn_proj_kernel(*refs):
    n_slabs = D_MODEL // LANE
    x_slabs = refs[:n_slabs]
    w_ref, wvt_ref, o_ref, vt_ref, xb_ref = refs[n_slabs:]
    for r in range(RADIX):
        xb_ref[r] = jnp.concatenate(
            [slab[pl.ds(r, PROJ_TM // RADIX, stride=RADIX), :] for slab in x_slabs], axis=1).astype(bf16)
    x = jnp.concatenate([slab[...] for slab in x_slabs], axis=1).astype(bf16)
    q_scale = HEAD_DIM ** -0.5 * LOG2E
    for j in range(ATT_COLS // PROJ_TN):
        cols = slice(j * PROJ_TN, (j + 1) * PROJ_TN)
        acc = jnp.dot(x, w_ref[:, cols], preferred_element_type=f32)
        if (j + 1) * PROJ_TN <= ATTN_WIDTH:
            acc = acc * q_scale
        o_ref[:, cols] = acc.astype(o_ref.dtype)
    vt = lax.dot_general(wvt_ref[...], x, (((1,), (1,)), ((), ())), preferred_element_type=f32)
    for j in range(PROJ_TM // BLOCK):
        vt_ref[j] = vt[:, j * BLOCK:(j + 1) * BLOCK].astype(vt_ref.dtype)


def _attn_proj(x2d, w_att, wvt):
    rows = x2d.shape[0]
    tiles_per_seq = SEQ // PROJ_TM
    n_slabs = D_MODEL // LANE
    return pl.pallas_call(
        _attn_proj_kernel,
        grid=(rows // PROJ_TM,),
        in_specs=[pl.BlockSpec((PROJ_TM, LANE), functools.partial(lambda j, i: (i, j), j))
                  for j in range(n_slabs)]
                 + [pl.BlockSpec((D_MODEL, ATT_COLS), lambda i: (0, 0)),
                    pl.BlockSpec((KV_WIDTH, D_MODEL), lambda i: (0, 0))],
        out_specs=[pl.BlockSpec((PROJ_TM, ATT_COLS), lambda i: (i, 0)),
                   pl.BlockSpec((PROJ_TM // BLOCK, KV_WIDTH, BLOCK), lambda i: (i, 0, 0)),
                   pl.BlockSpec((None, RADIX, PROJ_TM // RADIX, D_MODEL),
                                lambda i: (i // tiles_per_seq, 0, i % tiles_per_seq, 0))],
        out_shape=[jax.ShapeDtypeStruct((rows, ATT_COLS), bf16),
                   jax.ShapeDtypeStruct((rows // BLOCK, KV_WIDTH, BLOCK), bf16),
                   jax.ShapeDtypeStruct((rows // SEQ, RADIX, SUB, D_MODEL), bf16)],
        compiler_params=pltpu.CompilerParams(dimension_semantics=("arbitrary",),
                                             vmem_limit_bytes=VMEM_LIMIT),
        name="attn_proj",
    )(*([x2d] * n_slabs), w_att, wvt)


N_QBLOCKS = SEQ // BLOCK
BAND = 3 * BLOCK
GQ = GROUP * BLOCK


PAIR_ROWS = 2 * BLOCK
MERGE_TN = 512


def _attn_merge_kernel(relb_ref, sink_ref, bucket_ref, q_ref, g_ref, k_ref, vt_ref, x_ref, yh_ref,
                       wbr_ref, wa_ref, wh_ref, wo_ref, lng_ref, lnb_ref, o_ref,
                       bias_ref, ya_ref, br_ref, ph_ref, pa_ref):
    pair = pl.program_id(1)

    def mm(a, b):
        return jnp.dot(a, b, preferred_element_type=f32)

    @pl.when((pl.program_id(0) == 0) & (pair == 0))
    def _():
        bucket = bucket_ref[...]
        neg = jnp.full((BLOCK, GQ), NEG_INF, f32)
        for h in range(N_KV_HEADS):
            cols = []
            for g in range(GROUP):
                acc = jnp.full(bucket.shape, NEG_INF, f32)
                for kb in range(REL_BUCKETS):
                    acc = jnp.where(bucket == kb, relb_ref[kb, h * GROUP + g] * LOG2E, acc)
                cols.append(acc)
            tbl = jnp.concatenate(cols, axis=1)
            bias_ref[0, h] = tbl
            bias_ref[1, h] = jnp.concatenate([neg, tbl[BLOCK:]], axis=0)
            bias_ref[2, h] = jnp.concatenate([tbl[:2 * BLOCK], neg], axis=0)

    sink_rows = [
        jnp.concatenate([jnp.full((1, BLOCK), sink_ref[h * GROUP + g] * LOG2E, f32) for g in range(GROUP)], axis=1)
        for h in range(N_KV_HEADS)]

    xb = x_ref[...].astype(bf16)

    def gate_proj(c):
        def run(dep):
            cols = slice(c * MERGE_TN, (c + 1) * MERGE_TN)
            res = mm(xb, _anchor(wbr_ref[:, cols], dep))
            br_ref[:, cols] = 1.0 / (1.0 + jnp.exp(-res))
            return res
        return run

    def hyena_proj(c):
        def run(dep):
            cols = slice(c * MERGE_TN, (c + 1) * MERGE_TN)
            res = mm(yh_ref[...], _anchor(wh_ref[:, cols], dep))
            ph_ref[:, cols] = br_ref[:, D_MODEL + c * MERGE_TN:D_MODEL + (c + 1) * MERGE_TN] * res
            return res
        return run

    def attn_proj_pair(h0, accumulate):
        def run(dep):
            lo, hi = h0 * GROUP * HEAD_DIM, (h0 + 2) * GROUP * HEAD_DIM
            res = None
            for c in range(D_MODEL // MERGE_TN):
                cols = slice(c * MERGE_TN, (c + 1) * MERGE_TN)
                part = mm(ya_ref[:, lo:hi], _anchor(wa_ref[lo:hi, cols], dep))
                pa_ref[:, cols] = pa_ref[:, cols] + part if accumulate else part
                res = part
            return res
        return run

    fillers = [gate_proj(0), gate_proj(1), gate_proj(2), gate_proj(3), hyena_proj(0), hyena_proj(1),
               attn_proj_pair(0, False)]

    items = [(h, j) for h in range(N_KV_HEADS) for j in range(PAIR_ROWS // BLOCK)]

    def geometry(j):
        n = pair * (PAIR_ROWS // BLOCK) + j
        pj = jnp.maximum(n - 1, 0)
        nj = jnp.minimum(n + 1, N_QBLOCKS - 1)
        variant = jnp.where(n == 0, 1, jnp.where(n == N_QBLOCKS - 1, 2, 0))
        return n, pj, nj, variant

    def scores(h, j):
        n, pj, nj, _ = geometry(j)
        hd = slice(h * HEAD_DIM, (h + 1) * HEAD_DIM)
        blk = lambda i: k_ref[pl.ds(pl.multiple_of(i * BLOCK, BLOCK), BLOCK), hd]
        kband = jnp.concatenate([blk(pj), blk(n), blk(nj)], axis=0)
        rows = slice(j * BLOCK, (j + 1) * BLOCK)
        qs = jnp.concatenate(
            [q_ref[rows, (h * GROUP + g) * HEAD_DIM:(h * GROUP + g + 1) * HEAD_DIM] for g in range(GROUP)],
            axis=0)
        return lax.dot_general(kband, qs, (((1,), (1,)), ((), ())), preferred_element_type=f32)

    st_next = scores(*items[0])
    filled = None
    for idx, (h, j) in enumerate(items):
        n, pj, nj, variant = geometry(j)
        hd = slice(h * HEAD_DIM, (h + 1) * HEAD_DIM)
        rows = slice(j * BLOCK, (j + 1) * BLOCK)
        st = st_next + bias_ref[variant, h]
        ones = jnp.ones((PACK_ROWS, BAND), f32)
        if idx + 1 < len(items):
            st_next = scores(*items[idx + 1])
            ones = ones + _exact_zero(st_next[:PACK_ROWS, :BAND])
        if filled is not None:
            ones = ones + _exact_zero(filled[:PACK_ROWS, :BAND])
        filled = fillers[idx](_exact_zero(st[:PACK_ROWS, :])) if idx < len(fillers) else None
        vtband = jnp.concatenate([vt_ref[pj, hd, :], vt_ref[n, hd, :], vt_ref[nj, hd, :]], axis=1)
        vtband = jnp.concatenate([vtband, ones.astype(bf16)], axis=0)
        m = jnp.maximum(jnp.max(st, axis=0, keepdims=True), sink_rows[h])
        e = jnp.exp2(st - m)
        ot = mm(vtband, e.astype(bf16))
        den = ot[HEAD_DIM:HEAD_DIM + 1] + jnp.exp2(sink_rows[h] - m)
        ot = ot[:HEAD_DIM] * (1.0 / den)
        for p in range(GROUP // 2):
            two = jnp.concatenate([ot[:, (2 * p) * BLOCK:(2 * p + 1) * BLOCK],
                                   ot[:, (2 * p + 1) * BLOCK:(2 * p + 2) * BLOCK]], axis=0)
            c0 = (h * GROUP + 2 * p) * HEAD_DIM
            gate = g_ref[rows, c0:c0 + 2 * HEAD_DIM].astype(f32)
            ya_ref[rows, c0:c0 + 2 * HEAD_DIM] = (two.T * (gate / (1.0 + jnp.exp(-gate)))).astype(bf16)

    attn_proj_pair(2, True)(jnp.zeros((PACK_ROWS, MERGE_TN), f32))
    merged = br_ref[:, :D_MODEL] * pa_ref[...] + ph_ref[...]
    out = mm(merged.astype(bf16), wo_ref[...])
    r = DEEPNORM_ALPHA * x_ref[...] + out
    mu = jnp.mean(r, axis=-1, keepdims=True)
    d = r - mu
    var = jnp.mean(d * d, axis=-1, keepdims=True)
    o_ref[...] = d * lax.rsqrt(var + LN_EPS) * lng_ref[...] + lnb_ref[...]


def _attn_merge(u_att, vt, x2d, yh, rel_bias, sink, bucket_t, wbr, wa, wh, wo, ln_g, ln_b, batch):
    steps = SEQ // PAIR_ROWS
    row = lambda width, col=0: pl.BlockSpec((PAIR_ROWS, width), lambda b, p: (b * steps + p, col))
    full = lambda shape: pl.BlockSpec(shape, lambda b, p: (0,) * len(shape))
    return pl.pallas_call(
        _attn_merge_kernel,
        grid=(batch, steps),
        in_specs=[
            pl.BlockSpec(memory_space=pltpu.SMEM),
            pl.BlockSpec(memory_space=pltpu.SMEM),
            full((BAND, BLOCK)),
            row(ATTN_WIDTH, 0), row(ATTN_WIDTH, 1),
            pl.BlockSpec((SEQ, KV_WIDTH), lambda b, p: (b, 2 * ATTN_WIDTH // KV_WIDTH)),
            pl.BlockSpec((N_QBLOCKS, KV_WIDTH, BLOCK), lambda b, p: (b, 0, 0)),
            row(D_MODEL), row(HYENA_WIDTH),
            full((D_MODEL, 2 * D_MODEL)), full((ATTN_WIDTH, D_MODEL)), full((HYENA_WIDTH, D_MODEL)),
            full((D_MODEL, D_MODEL)), full((1, D_MODEL)), full((1, D_MODEL)),
        ],
        out_specs=row(D_MODEL),
        out_shape=jax.ShapeDtypeStruct((batch * SEQ, D_MODEL), f32),
        scratch_shapes=[pltpu.VMEM((3, N_KV_HEADS, BAND, GQ), f32), pltpu.VMEM((PAIR_ROWS, ATTN_WIDTH), bf16),
                        pltpu.VMEM((PAIR_ROWS, 2 * D_MODEL), f32), pltpu.VMEM((PAIR_ROWS, D_MODEL), f32),
                        pltpu.VMEM((PAIR_ROWS, D_MODEL), f32)],
        compiler_params=pltpu.CompilerParams(dimension_semantics=("arbitrary", "arbitrary"),
                                             vmem_limit_bytes=VMEM_LIMIT),
        name="attn_merge",
    )(rel_bias, sink, bucket_t, u_att, u_att, u_att, vt, x2d, yh, wbr, wa, wh, wo, ln_g, ln_b)


def _layer(x, w_in, rel_bias, attn_sink, conv_w, conv_b, filt_w1, filt_b1, filt_w2, filt_b2,
           filt_w3, filt_b3, filt_w4, filt_freq, hyena_skip, w_branch_attn, w_branch_hyena,
           w_out, ln_g, ln_b):
    batch = x.shape[0]
    wdft_np, vdft_np = _dft_matrices()
    feat_np, decay_np = _filter_constants()
    wdft = jnp.asarray(wdft_np, dtype=bf16)
    vdft = jnp.asarray(vdft_np, dtype=bf16)
    row2d = lambda v: v.reshape(1, -1).astype(f32)

    w1p = jnp.pad(filt_w1.astype(f32), ((0, FEAT_PAD - FILTER_EMB), (0, 0)))
    kre, kim = _filter_spectrum(w1p, row2d(filt_b1), filt_w2.astype(f32), row2d(filt_b2),
                                filt_w3.astype(f32), row2d(filt_b3), row2d(filt_freq),
                                filt_w4.astype(f32), jnp.asarray(feat_np), jnp.asarray(decay_np), wdft)

    w_in_b = w_in.astype(bf16)
    w_att = jnp.concatenate([w_in_b[:, COL_Q:COL_K], w_in_b[:, COL_AGATE:COL_HY],
                             w_in_b[:, COL_K:COL_V]], axis=1)
    wvt = w_in_b[:, COL_V:COL_AGATE].T
    x2d = x.reshape(batch * SEQ, D_MODEL)
    u_att, vt, xb = _attn_proj(x2d, w_att, wvt)
    y_h = _hyena_branch(xb.reshape(batch, SEQ, D_MODEL), w_in_b, conv_w.astype(f32), row2d(conv_b),
                        row2d(hyena_skip), kre, kim, wdft, vdft)
    out = _attn_merge(u_att, vt, x2d, y_h.reshape(batch * SEQ, HYENA_WIDTH), rel_bias.astype(f32),
                      attn_sink.astype(f32), jnp.asarray(_bucket_table()), w_in_b[:, COL_BR:],
                      w_branch_attn.astype(bf16), w_branch_hyena.astype(bf16), w_out.astype(bf16),
                      row2d(ln_g), row2d(ln_b), batch)
    return out.reshape(x.shape).astype(x.dtype)


def kernel(x, w_in, rel_bias, attn_sink, conv_w, conv_b, filt_w1, filt_b1, filt_w2, filt_b2,
           filt_w3, filt_b3, filt_w4, filt_freq, hyena_skip, w_branch_attn, w_branch_hyena,
           w_out, ln_g, ln_b):
    h = x
    for l in range(DEPTH):
        h = _layer(h, w_in[l], rel_bias, attn_sink[l], conv_w[l], conv_b[l],
                   filt_w1[l], filt_b1[l], filt_w2[l], filt_b2[l], filt_w3[l], filt_b3[l],
                   filt_w4[l], filt_freq[l], hyena_skip[l], w_branch_attn[l],
                   w_branch_hyena[l], w_out[l], ln_g[l], ln_b[l])
    return h
```

```python
import functools
import math

import numpy as np
import jax
import jax.numpy as jnp
from jax import lax
from jax.experimental import pallas as pl
from jax.experimental.pallas import tpu as pltpu

D_MODEL = 1024
SEQ = 2048
N_HEADS = 16
N_KV_HEADS = 4
HEAD_DIM = 64
GROUP = N_HEADS // N_KV_HEADS
ATTN_WIDTH = N_HEADS * HEAD_DIM
KV_WIDTH = N_KV_HEADS * HEAD_DIM
WINDOW = 128
BLOCK = 128
REL_BUCKETS = 32
REL_MAX_DIST = 128
NEG_INF = -1e30
HYENA_WIDTH = 1024
FILTER_EMB = 33
FILTER_HIDDEN = 64
DECAY_FAST = 0.3
DECAY_SLOW = 1.5
DECAY_TARGET = 1e-2
DEPTH = 1
DEEPNORM_ALPHA = (2 * DEPTH) ** 0.25
LN_EPS = 1e-5

COL_Q = 0
COL_K = ATTN_WIDTH
COL_V = COL_K + KV_WIDTH
COL_AGATE = COL_V + KV_WIDTH
COL_HY = COL_AGATE + ATTN_WIDTH
COL_HGATE = COL_HY + 3 * HYENA_WIDTH
COL_BR = COL_HGATE + HYENA_WIDTH
IN_COLS = COL_BR + 2 * D_MODEL

DFT_N = 2 * SEQ
RADIX = 4
SUB = SEQ // RADIX
CT = 256
LANE = 128
FEAT_PAD = 128

VMEM_LIMIT = 56 * 1024 * 1024

f32 = jnp.float32
bf16 = jnp.bfloat16


@functools.lru_cache(maxsize=None)
def _dft_matrices():
    f = np.arange(SUB, dtype=np.int64)[:, None]
    m = np.arange(SUB, dtype=np.int64)[None, :]
    w = np.empty((RADIX, 2 * SUB, SUB), np.float32)
    for r in range(RADIX):
        idx = ((2 * f + 1) * (RADIX * m + r)) % (2 * DFT_N)
        ang = np.pi * idx.astype(np.float64) / DFT_N
        w[r, :SUB] = np.cos(ang)
        w[r, SUB:] = -np.sin(ang)
    v = np.ascontiguousarray(np.transpose(w, (0, 2, 1))) * np.float32(2.0 / DFT_N)
    return w, v


@functools.lru_cache(maxsize=None)
def _filter_constants():
    bands = (FILTER_EMB - 1) // 2
    t = np.linspace(0.0, 1.0, SEQ, dtype=np.float32)[:, None]
    w = (2.0 * math.pi * np.arange(SEQ, dtype=np.float32)[:, None] / SEQ).astype(np.float32)
    fb = np.linspace(1e-4, bands - 1, bands, dtype=np.float32)[None, :]
    arg = (fb * w).astype(np.float64)
    feat = np.zeros((SEQ, FEAT_PAD), np.float32)
    feat[:, 0:1] = t
    feat[:, 1:1 + bands] = np.cos(arg)
    feat[:, 1 + bands:1 + 2 * bands] = -np.sin(arg)
    max_decay = math.log(DECAY_TARGET) / DECAY_FAST
    min_decay = math.log(DECAY_TARGET) / DECAY_SLOW
    deltas = np.linspace(min_decay, max_decay, HYENA_WIDTH, dtype=np.float32)
    decay = np.exp(-t.astype(np.float64) * np.abs(deltas.astype(np.float64))).astype(np.float32)
    return feat, decay


@functools.lru_cache(maxsize=None)
def _bucket_table():
    a = np.arange(BLOCK)[:, None]
    c = np.arange(3 * BLOCK)[None, :]
    rel = c - BLOCK - a
    half = REL_BUCKETS // 2
    max_exact = half // 2
    ret = (rel > 0).astype(np.int32) * half
    n = np.abs(rel)
    n_safe = np.maximum(n, 1).astype(np.float32)
    large = max_exact + (np.log(n_safe / max_exact) / math.log(REL_MAX_DIST / max_exact)
                         * (half - max_exact)).astype(np.int32)
    large = np.minimum(large, half - 1)
    bucket = (ret + np.where(n < max_exact, n, large)).astype(np.int32)
    return np.ascontiguousarray(np.where(np.abs(rel) <= WINDOW, bucket, -1).astype(np.int32).T)


def _store_lane_split(ref, val):
    for j in range(ref.shape[0]):
        ref[j] = val[:, j * LANE:(j + 1) * LANE]


def _load_lane_split(ref, rows=slice(None)):
    return jnp.concatenate([ref[j, rows, :] for j in range(ref.shape[0])], axis=1)


def _dft_forward(z_ref, w_ref):
    a = []
    for r in range(RADIX):
        zr = _load_lane_split(z_ref, pl.ds(r, SUB, stride=RADIX)).astype(bf16)
        ar = jnp.dot(w_ref[r], zr, preferred_element_type=f32)
        a.append((ar[:SUB], ar[SUB:]))
    s02 = (a[0][0] + a[2][0], a[0][1] + a[2][1])
    d02 = (a[0][0] - a[2][0], a[0][1] - a[2][1])
    s13 = (a[1][0] + a[3][0], a[1][1] + a[3][1])
    d13 = (a[1][0] - a[3][0], a[1][1] - a[3][1])
    x0 = (s02[0] + s13[0], s02[1] + s13[1])
    x2 = (s02[0] - s13[0], s02[1] - s13[1])
    x1 = (d02[0] + d13[1], d02[1] - d13[0])
    x3 = (d02[0] - d13[1], d02[1] + d13[0])
    return [x0, x1, x2, x3]


def _dft_inverse(p, v_ref, y_ref):
    s02 = (p[0][0] + p[2][0], p[0][1] + p[2][1])
    d02 = (p[0][0] - p[2][0], p[0][1] - p[2][1])
    s13 = (p[1][0] + p[3][0], p[1][1] + p[3][1])
    d13 = (p[1][0] - p[3][0], p[1][1] - p[3][1])
    q = [
        (s02[0] + s13[0], s02[1] + s13[1]),
        (d02[0] - d13[1], d02[1] + d13[0]),
        (s02[0] - s13[0], s02[1] - s13[1]),
        (d02[0] + d13[1], d02[1] - d13[0]),
    ]
    for r in range(RADIX):
        qr = jnp.concatenate([q[r][0], q[r][1]], axis=0).astype(bf16)
        yr = jnp.dot(v_ref[r], qr, preferred_element_type=f32)
        for j in range(y_ref.shape[0]):
            y_ref[j, pl.ds(r, SUB, stride=RADIX), :] = yr[:, j * LANE:(j + 1) * LANE]


def _filter_kernel(feat_ref, w1_ref, b1_ref, w2_ref, b2_ref, w3_ref, b3_ref, fr_ref,
                   w4f_ref, w4b_ref, decay_ref, wdft_ref, kre_ref, kim_ref, h3_ref, sig_ref):
    hi = lax.Precision.HIGHEST

    @pl.when(pl.program_id(0) == 0)
    def _():
        fr = fr_ref[...]
        h = jnp.sin(fr * (jnp.dot(feat_ref[...], w1_ref[...], precision=hi,
                                  preferred_element_type=f32) + b1_ref[...]))
        h = jnp.sin(fr * (jnp.dot(h, w2_ref[...], precision=hi, preferred_element_type=f32) + b2_ref[...]))
        h = jnp.sin(fr * (jnp.dot(h, w3_ref[...], precision=hi, preferred_element_type=f32) + b3_ref[...]))
        h3_ref[...] = h

    h3 = h3_ref[...]
    decay = decay_ref[...]
    hf = jnp.dot(h3, w4f_ref[...], precision=hi, preferred_element_type=f32) * decay
    hb = jnp.dot(h3, w4b_ref[...], precision=hi, preferred_element_type=f32) * decay
    row = lax.broadcasted_iota(jnp.int32, hb.shape, 0)
    hb = jnp.where(row == 0, 0.0, hb)
    _store_lane_split(sig_ref, hf + hb)
    xs = _dft_forward(sig_ref, wdft_ref)
    for k in range(RADIX):
        kre_ref[k * SUB:(k + 1) * SUB, :] = xs[k][0]
    _store_lane_split(sig_ref, hf - hb)
    xd = _dft_forward(sig_ref, wdft_ref)
    for k in range(RADIX):
        kim_ref[k * SUB:(k + 1) * SUB, :] = xd[k][1]


def _filter_spectrum(w1p, b1, w2, b2, w3, b3, fr, w4, feat, decay, wdft):
    nct = HYENA_WIDTH // CT
    full = lambda shape: pl.BlockSpec(shape, lambda c: (0,) * len(shape))
    return pl.pallas_call(
        _filter_kernel,
        grid=(nct,),
        in_specs=[
            full((SEQ, FEAT_PAD)), full((FEAT_PAD, FILTER_HIDDEN)), full((1, FILTER_HIDDEN)),
            full((FILTER_HIDDEN, FILTER_HIDDEN)), full((1, FILTER_HIDDEN)),
            full((FILTER_HIDDEN, FILTER_HIDDEN)), full((1, FILTER_HIDDEN)), full((1, FILTER_HIDDEN)),
            pl.BlockSpec((FILTER_HIDDEN, CT), lambda c: (0, c)),
            pl.BlockSpec((FILTER_HIDDEN, CT), lambda c: (0, nct + c)),
            pl.BlockSpec((SEQ, CT), lambda c: (0, c)),
            full((RADIX, 2 * SUB, SUB)),
        ],
        out_specs=[pl.BlockSpec((SEQ, CT), lambda c: (0, c)), pl.BlockSpec((SEQ, CT), lambda c: (0, c))],
        out_shape=[jax.ShapeDtypeStruct((SEQ, HYENA_WIDTH), f32)] * 2,
        scratch_shapes=[pltpu.VMEM((SEQ, FILTER_HIDDEN), f32), pltpu.VMEM((CT // LANE, SEQ, LANE), f32)],
        compiler_params=pltpu.CompilerParams(dimension_semantics=("arbitrary",),
                                             vmem_limit_bytes=VMEM_LIMIT),
        name="filter_spectrum",
    )(feat, w1p, b1, w2, b2, w3, b3, fr, w4, w4, decay, wdft)


CONV_ROWS = 512
HALF_ROWS = CONV_ROWS // 2
PIECE_ROWS = 64
PACK_ROWS = 16
SPEC_ROWS = PACK_ROWS


def _fold_rows(val, acc):
    piece = val[:PACK_ROWS]
    return piece if acc is None else acc + piece


def _exact_zero(fold):
    return jnp.minimum(jnp.abs(fold), 0.0)


def _anchor(w, zero):
    head = (w[:PACK_ROWS].astype(f32) + zero).astype(w.dtype)
    return jnp.concatenate([head, w[PACK_ROWS:]], axis=0)


def _fill_neighbours(ext_ref):
    row = lax.broadcasted_iota(jnp.int32, (SUB, CT), 0)
    last = ext_ref[SEQ - 1:SEQ + SUB - 1, :]
    ext_ref[0:SUB, :] = jnp.where(row == 0, 0.0, last)
    first = ext_ref[SUB + 1:2 * SUB + 1, :]
    ext_ref[SUB + SEQ:2 * SUB + SEQ, :] = jnp.where(row == SUB - 1, 0.0, first)


def _short_conv(ext_ref, i0, rows, cw_ref, cb):
    cw = cw_ref[...]
    return (cb + cw[0:1] * ext_ref[pl.ds(i0, rows), :] + cw[1:2] * ext_ref[pl.ds(SUB + i0, rows), :]
            + cw[2:3] * ext_ref[pl.ds(2 * SUB + i0, rows), :])


def _hyena_kernel(x_ref, wx0_ref, wx1_ref, wv_ref, wg_ref,
                  cw0_ref, cw1_ref, cwv_ref, cb0_ref, cb1_ref, cbv_ref, skip_ref,
                  kre_ref, kim_ref, wdft_ref, vdft_ref, o_ref,
                  e1_ref, ev_ref, e0_ref, gate_ref, z_ref, zb_ref, a_ref, q_ref):
    halves = CT // LANE

    def mm(a, b):
        return jnp.dot(a, b, preferred_element_type=f32)

    def conv_z(i0, rows):
        fold = None
        for c in range(i0, i0 + rows, PIECE_ROWS):
            z = (_short_conv(ev_ref, c, PIECE_ROWS, cwv_ref, cbv_ref[...])
                 * _short_conv(e1_ref, c, PIECE_ROWS, cw1_ref, cb1_ref[...]))
            z_ref[c:c + PIECE_ROWS, :] = z
            zb_ref[c:c + PIECE_ROWS, :] = z.astype(bf16)
            fold = _fold_rows(z, fold)
        return _exact_zero(fold)

    def gate_rows(i0, rows):
        fold = None
        for c in range(i0, i0 + rows, PIECE_ROWS):
            g = gate_ref[c:c + PIECE_ROWS, :]
            gated = _short_conv(e0_ref, c, PIECE_ROWS, cw0_ref, cb0_ref[...]) * (g / (1.0 + jnp.exp(-g)))
            gate_ref[c:c + PIECE_ROWS, :] = gated
            fold = _fold_rows(gated, fold)
        return _exact_zero(fold)

    def spectral(f_start, rows):
        fold = None
        for f0 in range(f_start, f_start + rows, SPEC_ROWS):
            def a_rows(r, base):
                return jnp.concatenate(
                    [a_ref[r * halves + j, base + f0:base + f0 + SPEC_ROWS, :] for j in range(halves)], axis=1)

            are = [a_rows(r, 0) for r in range(RADIX)]
            aim = [a_rows(r, SUB) for r in range(RADIX)]
            s02 = (are[0] + are[2], aim[0] + aim[2])
            d02 = (are[0] - are[2], aim[0] - aim[2])
            s13 = (are[1] + are[3], aim[1] + aim[3])
            d13 = (are[1] - are[3], aim[1] - aim[3])
            xs = [(s02[0] + s13[0], s02[1] + s13[1]),
                  (d02[0] + d13[1], d02[1] - d13[0]),
                  (s02[0] - s13[0], s02[1] - s13[1]),
                  (d02[0] - d13[1], d02[1] + d13[0])]
            p = []
            for k in range(RADIX):
                kre = kre_ref[k * SUB + f0:k * SUB + f0 + SPEC_ROWS, :]
                kim = kim_ref[k * SUB + f0:k * SUB + f0 + SPEC_ROWS, :]
                xr, xi = xs[k]
                p.append((xr * kre - xi * kim, xr * kim + xi * kre))
            s02 = (p[0][0] + p[2][0], p[0][1] + p[2][1])
            d02 = (p[0][0] - p[2][0], p[0][1] - p[2][1])
            s13 = (p[1][0] + p[3][0], p[1][1] + p[3][1])
            d13 = (p[1][0] - p[3][0], p[1][1] - p[3][1])
            qs = [(s02[0] + s13[0], s02[1] + s13[1]),
                  (d02[0] - d13[1], d02[1] + d13[0]),
                  (s02[0] - s13[0], s02[1] - s13[1]),
                  (d02[0] + d13[1], d02[1] - d13[0])]
            for r in range(RADIX):
                q_ref[r, f0:f0 + SPEC_ROWS, :] = qs[r][0].astype(bf16)
                q_ref[r, SUB + f0:SUB + f0 + SPEC_ROWS, :] = qs[r][1].astype(bf16)
                fold = _fold_rows(qs[r][0] + qs[r][1], fold)
        return _exact_zero(fold)

    def proj_rows(i0, dep):
        xj = x_ref[i0:i0 + HALF_ROWS, :]
        return mm(xj, _anchor(wx0_ref[...], dep)), mm(xj, _anchor(wg_ref[...], dep))

    def store_proj(i0, e0, g):
        e0_ref[SUB + i0:SUB + i0 + HALF_ROWS, :] = e0
        gate_ref[i0:i0 + HALF_ROWS, :] = g

    e1_ref[SUB:SUB + SEQ, :] = mm(x_ref[...], wx1_ref[...])
    ev_ref[SUB:SUB + SEQ, :] = mm(x_ref[...], wv_ref[...])
    _fill_neighbours(e1_ref)
    _fill_neighbours(ev_ref)

    n_pairs = SEQ // 2 // HALF_ROWS
    dep = jnp.zeros((PACK_ROWS, CT), f32)
    for k in range(n_pairs):
        store_proj(k * HALF_ROWS, *proj_rows(k * HALF_ROWS, dep))
        dep = conv_z(k * (SEQ // n_pairs), SEQ // n_pairs)

    for r in range(RADIX):
        ar = mm(wdft_ref[r], zb_ref[r * SUB:(r + 1) * SUB, :])
        for j in range(halves):
            a_ref[r * halves + j] = ar[:, j * LANE:(j + 1) * LANE]

    for k in range(n_pairs):
        i0 = SEQ // 2 + k * HALF_ROWS
        store_proj(i0, *proj_rows(i0, dep))
        dep = spectral(k * (SUB // n_pairs), SUB // n_pairs)
    _fill_neighbours(e0_ref)

    half_rows = SUB // 2
    for r in range(RADIX):
        blk = slice(r * SUB, (r + 1) * SUB)
        y = mm(vdft_ref[r], _anchor(q_ref[r], dep))
        dep = gate_rows(r * SUB, SUB)
        o = (y + z_ref[blk, :] * skip_ref[...]) * gate_ref[blk, :]
        for j in range(halves):
            for h in range(2):
                a_ref[j * 2 + h, pl.ds(r, half_rows, stride=RADIX), :] = (
                    o[h * half_rows:(h + 1) * half_rows, j * LANE:(j + 1) * LANE])
    o_ref[...] = jnp.concatenate(
        [jnp.concatenate([a_ref[j * 2 + h, 0:SEQ // 2, :] for h in range(2)], axis=0) for j in range(halves)],
        axis=1).astype(o_ref.dtype)


def _hyena_branch(xb, w_in_b, conv_w, conv_b, skip, kre, kim, wdft, vdft):
    batch = xb.shape[0]
    nct = HYENA_WIDTH // CT
    wcol = lambda base: pl.BlockSpec((D_MODEL, CT), lambda c, b: (0, base // CT + c))
    ccol = lambda rows, part: pl.BlockSpec((rows, CT), lambda c, b: (0, part * nct + c))
    tile = pl.BlockSpec((SEQ, CT), lambda c, b: (0, c))
    return pl.pallas_call(
        _hyena_kernel,
        grid=(nct, batch),
        in_specs=[
            pl.BlockSpec((None, SEQ, D_MODEL), lambda c, b: (b, 0, 0)),
            wcol(COL_HY), wcol(COL_HY + HYENA_WIDTH), wcol(COL_HY + 2 * HYENA_WIDTH), wcol(COL_HGATE),
            ccol(3, 0), ccol(3, 1), ccol(3, 2), ccol(1, 0), ccol(1, 1), ccol(1, 2),
            pl.BlockSpec((1, CT), lambda c, b: (0, c)),
            tile, tile,
            pl.BlockSpec((RADIX, 2 * SUB, SUB), lambda c, b: (0, 0, 0)),
            pl.BlockSpec((RADIX, SUB, 2 * SUB), lambda c, b: (0, 0, 0)),
        ],
        out_specs=pl.BlockSpec((None, SEQ, CT), lambda c, b: (b, 0, c)),
        out_shape=jax.ShapeDtypeStruct((batch, SEQ, HYENA_WIDTH), bf16),
        scratch_shapes=[pltpu.VMEM((SEQ + 2 * SUB, CT), f32), pltpu.VMEM((SEQ + 2 * SUB, CT), f32),
                        pltpu.VMEM((SEQ + 2 * SUB, CT), f32), pltpu.VMEM((SEQ, CT), f32),
                        pltpu.VMEM((SEQ, CT), f32), pltpu.VMEM((SEQ, CT), bf16),
                        pltpu.VMEM((RADIX * (CT // LANE), 2 * SUB, LANE), f32),
                        pltpu.VMEM((RADIX, 2 * SUB, CT), bf16)],
        compiler_params=pltpu.CompilerParams(dimension_semantics=("arbitrary", "arbitrary"),
                                             vmem_limit_bytes=VMEM_LIMIT),
        name="hyena_branch",
    )(xb, w_in_b, w_in_b, w_in_b, w_in_b, conv_w, conv_w, conv_w, conv_b, conv_b, conv_b,
      skip, kre, kim, wdft, vdft)


ATT_COLS = 2 * ATTN_WIDTH + KV_WIDTH
PROJ_TM = 1024
PROJ_TN = 256
LOG2E = math.log2(math.e)


def _attn_proj_kernel(*refs):
    n_slabs = D_MODEL // LANE
    x_slabs = refs[:n_slabs]
    w_ref, wvt_ref, o_ref, vt_ref, xb_ref = refs[n_slabs:]
    for r in range(RADIX):
        xb_ref[r] = jnp.concatenate(
            [slab[pl.ds(r, PROJ_TM // RADIX, stride=RADIX), :] for slab in x_slabs], axis=1).astype(bf16)
    x = jnp.concatenate([slab[...] for slab in x_slabs], axis=1).astype(bf16)
    q_scale = HEAD_DIM ** -0.5 * LOG2E
    for j in range(ATT_COLS // PROJ_TN):
        cols = slice(j * PROJ_TN, (j + 1) * PROJ_TN)
        acc = jnp.dot(x, w_ref[:, cols], preferred_element_type=f32)
        if (j + 1) * PROJ_TN <= ATTN_WIDTH:
            acc = acc * q_scale
        o_ref[:, cols] = acc.astype(o_ref.dtype)
    vt = lax.dot_general(wvt_ref[...], x, (((1,), (1,)), ((), ())), preferred_element_type=f32)
    for j in range(PROJ_TM // BLOCK):
        vt_ref[j] = vt[:, j * BLOCK:(j + 1) * BLOCK].astype(vt_ref.dtype)


def _attn_proj(x2d, w_att, wvt):
    rows = x2d.shape[0]
    tiles_per_seq = SEQ // PROJ_TM
    n_slabs = D_MODEL // LANE
    return pl.pallas_call(
        _attn_proj_kernel,
        grid=(rows // PROJ_TM,),
        in_specs=[pl.BlockSpec((PROJ_TM, LANE), functools.partial(lambda j, i: (i, j), j))
                  for j in range(n_slabs)]
                 + [pl.BlockSpec((D_MODEL, ATT_COLS), lambda i: (0, 0)),
                    pl.BlockSpec((KV_WIDTH, D_MODEL), lambda i: (0, 0))],
        out_specs=[pl.BlockSpec((PROJ_TM, ATT_COLS), lambda i: (i, 0)),
                   pl.BlockSpec((PROJ_TM // BLOCK, KV_WIDTH, BLOCK), lambda i: (i, 0, 0)),
                   pl.BlockSpec((None, RADIX, PROJ_TM // RADIX, D_MODEL),
                                lambda i: (i // tiles_per_seq, 0, i % tiles_per_seq, 0))],
        out_shape=[jax.ShapeDtypeStruct((rows, ATT_COLS), bf16),
                   jax.ShapeDtypeStruct((rows // BLOCK, KV_WIDTH, BLOCK), bf16),
                   jax.ShapeDtypeStruct((rows // SEQ, RADIX, SUB, D_MODEL), bf16)],
        compiler_params=pltpu.CompilerParams(dimension_semantics=("arbitrary",),
                                             vmem_limit_bytes=VMEM_LIMIT),
        name="attn_proj",
    )(*([x2d] * n_slabs), w_att, wvt)


N_QBLOCKS = SEQ // BLOCK
BAND = 3 * BLOCK
GQ = GROUP * BLOCK


PAIR_ROWS = 2 * BLOCK
PAIRS_PER_STEP = 2
MERGE_TN = 512


def _attn_merge_kernel(relb_ref, sink_ref, bucket_ref, q_ref, g_ref, k_ref, vt_ref, x_ref, yh_ref,
                       wbr_ref, wa_ref, wh_ref, wo_ref, lng_ref, lnb_ref, o_ref,
                       bias_ref, ya_ref, br_ref, ph_ref, pa_ref):
    pair = pl.program_id(1)

    def mm(a, b):
        return jnp.dot(a, b, preferred_element_type=f32)

    @pl.when((pl.program_id(0) == 0) & (pair == 0))
    def _():
        bucket = bucket_ref[...]
        neg = jnp.full((BLOCK, GQ), NEG_INF, f32)
        for h in range(N_KV_HEADS):
            cols = []
            for g in range(GROUP):
                acc = jnp.full(bucket.shape, NEG_INF, f32)
                for kb in range(REL_BUCKETS):
                    acc = jnp.where(bucket == kb, relb_ref[kb, h * GROUP + g] * LOG2E, acc)
                cols.append(acc)
            tbl = jnp.concatenate(cols, axis=1)
            bias_ref[0, h] = tbl
            bias_ref[1, h] = jnp.concatenate([neg, tbl[BLOCK:]], axis=0)
            bias_ref[2, h] = jnp.concatenate([tbl[:2 * BLOCK], neg], axis=0)

    sink_rows = [
        jnp.concatenate([jnp.full((1, BLOCK), sink_ref[h * GROUP + g] * LOG2E, f32) for g in range(GROUP)], axis=1)
        for h in range(N_KV_HEADS)]

    xb = x_ref[...].astype(bf16)
    blocks_per_pair = PAIR_ROWS // BLOCK

    def wide(dep):
        return jnp.concatenate([dep] * (D_MODEL // MERGE_TN), axis=1)

    def gate_proj(a, c):
        def run(dep):
            ra = slice(a * PAIR_ROWS, (a + 1) * PAIR_ROWS)
            cols = slice(c * MERGE_TN, (c + 1) * MERGE_TN)
            res = mm(xb[ra], _anchor(wbr_ref[:, cols], dep))
            br_ref[ra, cols] = 1.0 / (1.0 + jnp.exp(-res))
            return res
        return run

    def hyena_proj(a, c):
        def run(dep):
            ra = slice(a * PAIR_ROWS, (a + 1) * PAIR_ROWS)
            cols = slice(c * MERGE_TN, (c + 1) * MERGE_TN)
            res = mm(yh_ref[ra, :], _anchor(wh_ref[:, cols], dep))
            ph_ref[ra, cols] = br_ref[ra, D_MODEL + c * MERGE_TN:D_MODEL + (c + 1) * MERGE_TN] * res
            return res
        return run

    def attn_proj_pair(a, h0, accumulate):
        def run(dep):
            ra = slice(a * PAIR_ROWS, (a + 1) * PAIR_ROWS)
            lo, hi = h0 * GROUP * HEAD_DIM, (h0 + 2) * GROUP * HEAD_DIM
            res = None
            for c in range(D_MODEL // MERGE_TN):
                cols = slice(c * MERGE_TN, (c + 1) * MERGE_TN)
                part = mm(ya_ref[ra, lo:hi], _anchor(wa_ref[lo:hi, cols], dep))
                pa_ref[ra, cols] = pa_ref[ra, cols] + part if accumulate else part
                res = part
            return res
        return run

    def out_proj(a):
        def run(dep):
            ra = slice(a * PAIR_ROWS, (a + 1) * PAIR_ROWS)
            merged = br_ref[ra, :D_MODEL] * pa_ref[ra, :] + ph_ref[ra, :]
            out = mm(merged.astype(bf16), _anchor(wo_ref[...], wide(dep)))
            pa_ref[ra, :] = out
            return out[:, :MERGE_TN]
        return run

    def layer_norm(a):
        def run(dep):
            del dep
            ra = slice(a * PAIR_ROWS, (a + 1) * PAIR_ROWS)
            r = DEEPNORM_ALPHA * x_ref[ra, :] + pa_ref[ra, :]
            mu = jnp.mean(r, axis=-1, keepdims=True)
            d = r - mu
            var = jnp.mean(d * d, axis=-1, keepdims=True)
            y = d * lax.rsqrt(var + LN_EPS) * lng_ref[...] + lnb_ref[...]
            o_ref[ra, :] = y
            return y[:, :MERGE_TN]
        return run

    items = [(a, h, j) for a in range(PAIRS_PER_STEP) for h in range(N_KV_HEADS) for j in range(blocks_per_pair)]
    per_pair = N_KV_HEADS * blocks_per_pair

    slots = [[] for _ in items]
    for a in range(PAIRS_PER_STEP):
        if a == 0:
            plan = {0: [gate_proj(a, 0)], 1: [gate_proj(a, 1)], 2: [gate_proj(a, 2)], 3: [gate_proj(a, 3)]}
        else:
            plan = {-1: [gate_proj(a, 0)], 0: [attn_proj_pair(a - 1, 2, True), gate_proj(a, 1)],
                    1: [out_proj(a - 1)], 2: [gate_proj(a, 2)], 3: [layer_norm(a - 1), gate_proj(a, 3)]}
        plan.update({4: [hyena_proj(a, 0)], 5: [hyena_proj(a, 1)], 6: [attn_proj_pair(a, 0, False)]})
        for k, ops in plan.items():
            slots[a * per_pair + k].extend(ops)

    def geometry(a, j):
        n = (pair * PAIRS_PER_STEP + a) * blocks_per_pair + j
        pj = jnp.maximum(n - 1, 0)
        nj = jnp.minimum(n + 1, N_QBLOCKS - 1)
        variant = jnp.where(n == 0, 1, jnp.where(n == N_QBLOCKS - 1, 2, 0))
        return n, pj, nj, variant

    def scores(a, h, j):
        n, pj, nj, _ = geometry(a, j)
        hd = slice(h * HEAD_DIM, (h + 1) * HEAD_DIM)
        blk = lambda i: k_ref[pl.ds(pl.multiple_of(i * BLOCK, BLOCK), BLOCK), hd]
        kband = jnp.concatenate([blk(pj), blk(n), blk(nj)], axis=0)
        r0 = a * PAIR_ROWS + j * BLOCK
        qs = jnp.concatenate(
            [q_ref[r0:r0 + BLOCK, (h * GROUP + g) * HEAD_DIM:(h * GROUP + g + 1) * HEAD_DIM] for g in range(GROUP)],
            axis=0)
        return lax.dot_general(kband, qs, (((1,), (1,)), ((), ())), preferred_element_type=f32)

    st_next = scores(*items[0])
    filled = []
    for idx, (a, h, j) in enumerate(items):
        n, pj, nj, variant = geometry(a, j)
        hd = slice(h * HEAD_DIM, (h + 1) * HEAD_DIM)
        rows = slice(a * PAIR_ROWS + j * BLOCK, a * PAIR_ROWS + (j + 1) * BLOCK)
        st = st_next + bias_ref[variant, h]
        ones = jnp.ones((PACK_ROWS, BAND), f32)
        if idx + 1 < len(items):
            st_next = scores(*items[idx + 1])
            ones = ones + _exact_zero(st_next[:PACK_ROWS, :BAND])
        for res in filled:
            ones = ones + _exact_zero(res[:PACK_ROWS, :BAND])
        started = _exact_zero(st[:PACK_ROWS, :])
        filled = [op(started) for op in slots[idx]]
        vtband = jnp.concatenate([vt_ref[pj, hd, :], vt_ref[n, hd, :], vt_ref[nj, hd, :]], axis=1)
        vtband = jnp.concatenate([vtband, ones.astype(bf16)], axis=0)
        m = jnp.maximum(jnp.max(st, axis=0, keepdims=True), sink_rows[h])
        e = jnp.exp2(st - m)
        ot = mm(vtband, e.astype(bf16))
        den = ot[HEAD_DIM:HEAD_DIM + 1] + jnp.exp2(sink_rows[h] - m)
        ot = ot[:HEAD_DIM] * (1.0 / den)
        for p in range(GROUP // 2):
            two = jnp.concatenate([ot[:, (2 * p) * BLOCK:(2 * p + 1) * BLOCK],
                                   ot[:, (2 * p + 1) * BLOCK:(2 * p + 2) * BLOCK]], axis=0)
            c0 = (h * GROUP + 2 * p) * HEAD_DIM
            gate = g_ref[rows, c0:c0 + 2 * HEAD_DIM].astype(f32)
            ya_ref[rows, c0:c0 + 2 * HEAD_DIM] = (two.T * (gate / (1.0 + jnp.exp(-gate)))).astype(bf16)

    no_dep = jnp.zeros((PACK_ROWS, MERGE_TN), f32)
    last = PAIRS_PER_STEP - 1
    attn_proj_pair(last, 2, True)(no_dep)
    out_proj(last)(no_dep)
    layer_norm(last)(no_dep)


def _attn_merge(u_att, vt, x2d, yh, rel_bias, sink, bucket_t, wbr, wa, wh, wo, ln_g, ln_b, batch):
    step_rows = PAIRS_PER_STEP * PAIR_ROWS
    steps = SEQ // step_rows
    row = lambda width, col=0: pl.BlockSpec((step_rows, width), lambda b, p: (b * steps + p, col))
    full = lambda shape: pl.BlockSpec(shape, lambda b, p: (0,) * len(shape))
    return pl.pallas_call(
        _attn_merge_kernel,
        grid=(batch, steps),
        in_specs=[
            pl.BlockSpec(memory_space=pltpu.SMEM),
            pl.BlockSpec(memory_space=pltpu.SMEM),
            full((BAND, BLOCK)),
            row(ATTN_WIDTH, 0), row(ATTN_WIDTH, 1),
            pl.BlockSpec((SEQ, KV_WIDTH), lambda b, p: (b, 2 * ATTN_WIDTH // KV_WIDTH)),
            pl.BlockSpec((N_QBLOCKS, KV_WIDTH, BLOCK), lambda b, p: (b, 0, 0)),
            row(D_MODEL), row(HYENA_WIDTH),
            full((D_MODEL, 2 * D_MODEL)), full((ATTN_WIDTH, D_MODEL)), full((HYENA_WIDTH, D_MODEL)),
            full((D_MODEL, D_MODEL)), full((1, D_MODEL)), full((1, D_MODEL)),
        ],
        out_specs=row(D_MODEL),
        out_shape=jax.ShapeDtypeStruct((batch * SEQ, D_MODEL), f32),
        scratch_shapes=[pltpu.VMEM((3, N_KV_HEADS, BAND, GQ), f32), pltpu.VMEM((step_rows, ATTN_WIDTH), bf16),
                        pltpu.VMEM((step_rows, 2 * D_MODEL), f32), pltpu.VMEM((step_rows, D_MODEL), f32),
                        pltpu.VMEM((step_rows, D_MODEL), f32)],
        compiler_params=pltpu.CompilerParams(dimension_semantics=("arbitrary", "arbitrary"),
                                             vmem_limit_bytes=VMEM_LIMIT),
        name="attn_merge",
    )(rel_bias, sink, bucket_t, u_att, u_att, u_att, vt, x2d, yh, wbr, wa, wh, wo, ln_g, ln_b)


def _layer(x, w_in, rel_bias, attn_sink, conv_w, conv_b, filt_w1, filt_b1, filt_w2, filt_b2,
           filt_w3, filt_b3, filt_w4, filt_freq, hyena_skip, w_branch_attn, w_branch_hyena,
           w_out, ln_g, ln_b):
    batch = x.shape[0]
    wdft_np, vdft_np = _dft_matrices()
    feat_np, decay_np = _filter_constants()
    wdft = jnp.asarray(wdft_np, dtype=bf16)
    vdft = jnp.asarray(vdft_np, dtype=bf16)
    row2d = lambda v: v.reshape(1, -1).astype(f32)

    w1p = jnp.pad(filt_w1.astype(f32), ((0, FEAT_PAD - FILTER_EMB), (0, 0)))
    kre, kim = _filter_spectrum(w1p, row2d(filt_b1), filt_w2.astype(f32), row2d(filt_b2),
                                filt_w3.astype(f32), row2d(filt_b3), row2d(filt_freq),
                                filt_w4.astype(f32), jnp.asarray(feat_np), jnp.asarray(decay_np), wdft)

    w_in_b = w_in.astype(bf16)
    w_att = jnp.concatenate([w_in_b[:, COL_Q:COL_K], w_in_b[:, COL_AGATE:COL_HY],
                             w_in_b[:, COL_K:COL_V]], axis=1)
    wvt = w_in_b[:, COL_V:COL_AGATE].T
    x2d = x.reshape(batch * SEQ, D_MODEL)
    u_att, vt, xb = _attn_proj(x2d, w_att, wvt)
    y_h = _hyena_branch(xb.reshape(batch, SEQ, D_MODEL), w_in_b, conv_w.astype(f32), row2d(conv_b),
                        row2d(hyena_skip), kre, kim, wdft, vdft)
    out = _attn_merge(u_att, vt, x2d, y_h.reshape(batch * SEQ, HYENA_WIDTH), rel_bias.astype(f32),
                      attn_sink.astype(f32), jnp.asarray(_bucket_table()), w_in_b[:, COL_BR:],
                      w_branch_attn.astype(bf16), w_branch_hyena.astype(bf16), w_out.astype(bf16),
                      row2d(ln_g), row2d(ln_b), batch)
    return out.reshape(x.shape).astype(x.dtype)


def kernel(x, w_in, rel_bias, attn_sink, conv_w, conv_b, filt_w1, filt_b1, filt_w2, filt_b2,
           filt_w3, filt_b3, filt_w4, filt_freq, hyena_skip, w_branch_attn, w_branch_hyena,
           w_out, ln_g, ln_b):
    h = x
    for l in range(DEPTH):
        h = _layer(h, w_in[l], rel_bias, attn_sink[l], conv_w[l], conv_b[l],
                   filt_w1[l], filt_b1[l], filt_w2[l], filt_b2[l], filt_w3[l], filt_b3[l],
                   filt_w4[l], filt_freq[l], hyena_skip[l], w_branch_attn[l],
                   w_branch_hyena[l], w_out[l], ln_g[l], ln_b[l])
    return h
```

```python
import functools
import math

import numpy as np
import jax
import jax.numpy as jnp
from jax import lax
from jax.experimental import pallas as pl
from jax.experimental.pallas import tpu as pltpu

D_MODEL = 1024
SEQ = 2048
N_HEADS = 16
N_KV_HEADS = 4
HEAD_DIM = 64
GROUP = N_HEADS // N_KV_HEADS
ATTN_WIDTH = N_HEADS * HEAD_DIM
KV_WIDTH = N_KV_HEADS * HEAD_DIM
WINDOW = 128
BLOCK = 128
REL_BUCKETS = 32
REL_MAX_DIST = 128
NEG_INF = -1e30
HYENA_WIDTH = 1024
FILTER_EMB = 33
FILTER_HIDDEN = 64
DECAY_FAST = 0.3
DECAY_SLOW = 1.5
DECAY_TARGET = 1e-2
DEPTH = 1
DEEPNORM_ALPHA = (2 * DEPTH) ** 0.25
LN_EPS = 1e-5

COL_Q = 0
COL_K = ATTN_WIDTH
COL_V = COL_K + KV_WIDTH
COL_AGATE = COL_V + KV_WIDTH
COL_HY = COL_AGATE + ATTN_WIDTH
COL_HGATE = COL_HY + 3 * HYENA_WIDTH
COL_BR = COL_HGATE + HYENA_WIDTH
IN_COLS = COL_BR + 2 * D_MODEL

DFT_N = 2 * SEQ
RADIX = 4
SUB = SEQ // RADIX
CT = 256
LANE = 128
FEAT_PAD = 128

VMEM_LIMIT = 56 * 1024 * 1024

f32 = jnp.float32
bf16 = jnp.bfloat16


@functools.lru_cache(maxsize=None)
def _dft_matrices():
    f = np.arange(SUB, dtype=np.int64)[:, None]
    m = np.arange(SUB, dtype=np.int64)[None, :]
    w = np.empty((RADIX, 2 * SUB, SUB), np.float32)
    for r in range(RADIX):
        idx = ((2 * f + 1) * (RADIX * m + r)) % (2 * DFT_N)
        ang = np.pi * idx.astype(np.float64) / DFT_N
        w[r, :SUB] = np.cos(ang)
        w[r, SUB:] = -np.sin(ang)
    v = np.ascontiguousarray(np.transpose(w, (0, 2, 1))) * np.float32(2.0 / DFT_N)
    return w, v


@functools.lru_cache(maxsize=None)
def _filter_constants():
    bands = (FILTER_EMB - 1) // 2
    t = np.linspace(0.0, 1.0, SEQ, dtype=np.float32)[:, None]
    w = (2.0 * math.pi * np.arange(SEQ, dtype=np.float32)[:, None] / SEQ).astype(np.float32)
    fb = np.linspace(1e-4, bands - 1, bands, dtype=np.float32)[None, :]
    arg = (fb * w).astype(np.float64)
    feat = np.zeros((SEQ, FEAT_PAD), np.float32)
    feat[:, 0:1] = t
    feat[:, 1:1 + bands] = np.cos(arg)
    feat[:, 1 + bands:1 + 2 * bands] = -np.sin(arg)
    max_decay = math.log(DECAY_TARGET) / DECAY_FAST
    min_decay = math.log(DECAY_TARGET) / DECAY_SLOW
    deltas = np.linspace(min_decay, max_decay, HYENA_WIDTH, dtype=np.float32)
    decay = np.exp(-t.astype(np.float64) * np.abs(deltas.astype(np.float64))).astype(np.float32)
    return feat, decay


@functools.lru_cache(maxsize=None)
def _bucket_table():
    a = np.arange(BLOCK)[:, None]
    c = np.arange(3 * BLOCK)[None, :]
    rel = c - BLOCK - a
    half = REL_BUCKETS // 2
    max_exact = half // 2
    ret = (rel > 0).astype(np.int32) * half
    n = np.abs(rel)
    n_safe = np.maximum(n, 1).astype(np.float32)
    large = max_exact + (np.log(n_safe / max_exact) / math.log(REL_MAX_DIST / max_exact)
                         * (half - max_exact)).astype(np.int32)
    large = np.minimum(large, half - 1)
    bucket = (ret + np.where(n < max_exact, n, large)).astype(np.int32)
    return np.ascontiguousarray(np.where(np.abs(rel) <= WINDOW, bucket, -1).astype(np.int32).T)


def _store_lane_split(ref, val):
    for j in range(ref.shape[0]):
        ref[j] = val[:, j * LANE:(j + 1) * LANE]


def _load_lane_split(ref, rows=slice(None)):
    return jnp.concatenate([ref[j, rows, :] for j in range(ref.shape[0])], axis=1)


def _dft_forward(z_ref, w_ref):
    a = []
    for r in range(RADIX):
        zr = _load_lane_split(z_ref, pl.ds(r, SUB, stride=RADIX)).astype(bf16)
        ar = jnp.dot(w_ref[r], zr, preferred_element_type=f32)
        a.append((ar[:SUB], ar[SUB:]))
    s02 = (a[0][0] + a[2][0], a[0][1] + a[2][1])
    d02 = (a[0][0] - a[2][0], a[0][1] - a[2][1])
    s13 = (a[1][0] + a[3][0], a[1][1] + a[3][1])
    d13 = (a[1][0] - a[3][0], a[1][1] - a[3][1])
    x0 = (s02[0] + s13[0], s02[1] + s13[1])
    x2 = (s02[0] - s13[0], s02[1] - s13[1])
    x1 = (d02[0] + d13[1], d02[1] - d13[0])
    x3 = (d02[0] - d13[1], d02[1] + d13[0])
    return [x0, x1, x2, x3]


def _dft_inverse(p, v_ref, y_ref):
    s02 = (p[0][0] + p[2][0], p[0][1] + p[2][1])
    d02 = (p[0][0] - p[2][0], p[0][1] - p[2][1])
    s13 = (p[1][0] + p[3][0], p[1][1] + p[3][1])
    d13 = (p[1][0] - p[3][0], p[1][1] - p[3][1])
    q = [
        (s02[0] + s13[0], s02[1] + s13[1]),
        (d02[0] - d13[1], d02[1] + d13[0]),
        (s02[0] - s13[0], s02[1] - s13[1]),
        (d02[0] + d13[1], d02[1] - d13[0]),
    ]
    for r in range(RADIX):
        qr = jnp.concatenate([q[r][0], q[r][1]], axis=0).astype(bf16)
        yr = jnp.dot(v_ref[r], qr, preferred_element_type=f32)
        for j in range(y_ref.shape[0]):
            y_ref[j, pl.ds(r, SUB, stride=RADIX), :] = yr[:, j * LANE:(j + 1) * LANE]


def _filter_kernel(feat_ref, w1_ref, b1_ref, w2_ref, b2_ref, w3_ref, b3_ref, fr_ref,
                   w4f_ref, w4b_ref, decay_ref, wdft_ref, kre_ref, kim_ref, h3_ref, sig_ref):
    hi = lax.Precision.HIGHEST

    @pl.when(pl.program_id(0) == 0)
    def _():
        fr = fr_ref[...]
        h = jnp.sin(fr * (jnp.dot(feat_ref[...], w1_ref[...], precision=hi,
                                  preferred_element_type=f32) + b1_ref[...]))
        h = jnp.sin(fr * (jnp.dot(h, w2_ref[...], precision=hi, preferred_element_type=f32) + b2_ref[...]))
        h = jnp.sin(fr * (jnp.dot(h, w3_ref[...], precision=hi, preferred_element_type=f32) + b3_ref[...]))
        h3_ref[...] = h

    h3 = h3_ref[...]
    decay = decay_ref[...]
    hf = jnp.dot(h3, w4f_ref[...], precision=hi, preferred_element_type=f32) * decay
    hb = jnp.dot(h3, w4b_ref[...], precision=hi, preferred_element_type=f32) * decay
    row = lax.broadcasted_iota(jnp.int32, hb.shape, 0)
    hb = jnp.where(row == 0, 0.0, hb)
    _store_lane_split(sig_ref, hf + hb)
    xs = _dft_forward(sig_ref, wdft_ref)
    for k in range(RADIX):
        kre_ref[k * SUB:(k + 1) * SUB, :] = xs[k][0]
    _store_lane_split(sig_ref, hf - hb)
    xd = _dft_forward(sig_ref, wdft_ref)
    for k in range(RADIX):
        kim_ref[k * SUB:(k + 1) * SUB, :] = xd[k][1]


def _filter_spectrum(w1p, b1, w2, b2, w3, b3, fr, w4, feat, decay, wdft):
    nct = HYENA_WIDTH // CT
    full = lambda shape: pl.BlockSpec(shape, lambda c: (0,) * len(shape))
    return pl.pallas_call(
        _filter_kernel,
        grid=(nct,),
        in_specs=[
            full((SEQ, FEAT_PAD)), full((FEAT_PAD, FILTER_HIDDEN)), full((1, FILTER_HIDDEN)),
            full((FILTER_HIDDEN, FILTER_HIDDEN)), full((1, FILTER_HIDDEN)),
            full((FILTER_HIDDEN, FILTER_HIDDEN)), full((1, FILTER_HIDDEN)), full((1, FILTER_HIDDEN)),
            pl.BlockSpec((FILTER_HIDDEN, CT), lambda c: (0, c)),
            pl.BlockSpec((FILTER_HIDDEN, CT), lambda c: (0, nct + c)),
            pl.BlockSpec((SEQ, CT), lambda c: (0, c)),
            full((RADIX, 2 * SUB, SUB)),
        ],
        out_specs=[pl.BlockSpec((SEQ, CT), lambda c: (0, c)), pl.BlockSpec((SEQ, CT), lambda c: (0, c))],
        out_shape=[jax.ShapeDtypeStruct((SEQ, HYENA_WIDTH), f32)] * 2,
        scratch_shapes=[pltpu.VMEM((SEQ, FILTER_HIDDEN), f32), pltpu.VMEM((CT // LANE, SEQ, LANE), f32)],
        compiler_params=pltpu.CompilerParams(dimension_semantics=("arbitrary",),
                                             vmem_limit_bytes=VMEM_LIMIT),
        name="filter_spectrum",
    )(feat, w1p, b1, w2, b2, w3, b3, fr, w4, w4, decay, wdft)


HALF_ROWS = 512
PIECE_ROWS = 128
PACK_ROWS = 16
SPEC_ROWS = PACK_ROWS


def _fold_rows(val, acc):
    piece = val[:PACK_ROWS]
    return piece if acc is None else acc + piece


def _exact_zero(fold):
    return jnp.minimum(jnp.abs(fold), 0.0)


def _anchor(w, zero):
    head = (w[:PACK_ROWS].astype(f32) + zero).astype(w.dtype)
    return jnp.concatenate([head, w[PACK_ROWS:]], axis=0)


def _fill_neighbours(ext_ref):
    row = lax.broadcasted_iota(jnp.int32, (SUB, CT), 0)
    last = ext_ref[SEQ - 1:SEQ + SUB - 1, :]
    ext_ref[0:SUB, :] = jnp.where(row == 0, 0.0, last)
    first = ext_ref[SUB + 1:2 * SUB + 1, :]
    ext_ref[SUB + SEQ:2 * SUB + SEQ, :] = jnp.where(row == SUB - 1, 0.0, first)


def _short_conv(ext_ref, i0, rows, cw_ref, cb):
    cw = cw_ref[...]
    return (cb + cw[0:1] * ext_ref[pl.ds(i0, rows), :] + cw[1:2] * ext_ref[pl.ds(SUB + i0, rows), :]
            + cw[2:3] * ext_ref[pl.ds(2 * SUB + i0, rows), :])


def _hyena_kernel(x_ref, wx0_ref, wx1_ref, wv_ref, wg_ref,
                  cw0_ref, cw1_ref, cwv_ref, cb0_ref, cb1_ref, cbv_ref, skip_ref,
                  kre_ref, kim_ref, wdft_ref, vdft_ref, o_ref,
                  e1_ref, ev_ref, e0_ref, gate_ref, z_ref, zb_ref, a_ref, q_ref):
    halves = CT // LANE

    def mm(a, b):
        return jnp.dot(a, b, preferred_element_type=f32)

    def conv_z(i0, rows):
        fold = None
        for c in range(i0, i0 + rows, PIECE_ROWS):
            z = (_short_conv(ev_ref, c, PIECE_ROWS, cwv_ref, cbv_ref[...])
                 * _short_conv(e1_ref, c, PIECE_ROWS, cw1_ref, cb1_ref[...]))
            z_ref[c:c + PIECE_ROWS, :] = z
            zb_ref[c:c + PIECE_ROWS, :] = z.astype(bf16)
            fold = _fold_rows(z, fold)
        return _exact_zero(fold)

    def gate_rows(i0, rows):
        fold = None
        for c in range(i0, i0 + rows, PIECE_ROWS):
            g = gate_ref[c:c + PIECE_ROWS, :]
            gated = _short_conv(e0_ref, c, PIECE_ROWS, cw0_ref, cb0_ref[...]) * (g / (1.0 + jnp.exp(-g)))
            gate_ref[c:c + PIECE_ROWS, :] = gated
            fold = _fold_rows(gated, fold)
        return _exact_zero(fold)

    def spectral(f_start, rows):
        fold = None
        for f0 in range(f_start, f_start + rows, SPEC_ROWS):
            def a_rows(r, base):
                return jnp.concatenate(
                    [a_ref[r * halves + j, base + f0:base + f0 + SPEC_ROWS, :] for j in range(halves)], axis=1)

            are = [a_rows(r, 0) for r in range(RADIX)]
            aim = [a_rows(r, SUB) for r in range(RADIX)]
            s02 = (are[0] + are[2], aim[0] + aim[2])
            d02 = (are[0] - are[2], aim[0] - aim[2])
            s13 = (are[1] + are[3], aim[1] + aim[3])
            d13 = (are[1] - are[3], aim[1] - aim[3])
            xs = [(s02[0] + s13[0], s02[1] + s13[1]),
                  (d02[0] + d13[1], d02[1] - d13[0]),
                  (s02[0] - s13[0], s02[1] - s13[1]),
                  (d02[0] - d13[1], d02[1] + d13[0])]
            p = []
            for k in range(RADIX):
                kre = kre_ref[k * SUB + f0:k * SUB + f0 + SPEC_ROWS, :]
                kim = kim_ref[k * SUB + f0:k * SUB + f0 + SPEC_ROWS, :]
                xr, xi = xs[k]
                p.append((xr * kre - xi * kim, xr * kim + xi * kre))
            s02 = (p[0][0] + p[2][0], p[0][1] + p[2][1])
            d02 = (p[0][0] - p[2][0], p[0][1] - p[2][1])
            s13 = (p[1][0] + p[3][0], p[1][1] + p[3][1])
            d13 = (p[1][0] - p[3][0], p[1][1] - p[3][1])
            qs = [(s02[0] + s13[0], s02[1] + s13[1]),
                  (d02[0] - d13[1], d02[1] + d13[0]),
                  (s02[0] - s13[0], s02[1] - s13[1]),
                  (d02[0] + d13[1], d02[1] - d13[0])]
            for r in range(RADIX):
                q_ref[r, f0:f0 + SPEC_ROWS, :] = qs[r][0].astype(bf16)
                q_ref[r, SUB + f0:SUB + f0 + SPEC_ROWS, :] = qs[r][1].astype(bf16)
                fold = _fold_rows(qs[r][0] + qs[r][1], fold)
        return _exact_zero(fold)

    def proj_rows(i0, dep):
        xj = x_ref[i0:i0 + HALF_ROWS, :]
        return mm(xj, _anchor(wx0_ref[...], dep)), mm(xj, _anchor(wg_ref[...], dep))

    def store_proj(i0, e0, g):
        e0_ref[SUB + i0:SUB + i0 + HALF_ROWS, :] = e0
        gate_ref[i0:i0 + HALF_ROWS, :] = g

    e1_ref[SUB:SUB + SEQ, :] = mm(x_ref[...], wx1_ref[...])
    ev_ref[SUB:SUB + SEQ, :] = mm(x_ref[...], wv_ref[...])
    _fill_neighbours(e1_ref)
    _fill_neighbours(ev_ref)

    n_pairs = SEQ // 2 // HALF_ROWS
    dep = jnp.zeros((PACK_ROWS, CT), f32)
    for k in range(n_pairs):
        store_proj(k * HALF_ROWS, *proj_rows(k * HALF_ROWS, dep))
        dep = conv_z(k * (SEQ // n_pairs), SEQ // n_pairs)

    for r in range(RADIX):
        ar = mm(wdft_ref[r], zb_ref[r * SUB:(r + 1) * SUB, :])
        for j in range(halves):
            a_ref[r * halves + j] = ar[:, j * LANE:(j + 1) * LANE]

    for k in range(n_pairs):
        i0 = SEQ // 2 + k * HALF_ROWS
        store_proj(i0, *proj_rows(i0, dep))
        dep = spectral(k * (SUB // n_pairs), SUB // n_pairs)
    _fill_neighbours(e0_ref)

    half_rows = SUB // 2
    for r in range(RADIX):
        blk = slice(r * SUB, (r + 1) * SUB)
        y = mm(vdft_ref[r], _anchor(q_ref[r], dep))
        dep = gate_rows(r * SUB, SUB)
        o = (y + z_ref[blk, :] * skip_ref[...]) * gate_ref[blk, :]
        for j in range(halves):
            for h in range(2):
                a_ref[j * 2 + h, pl.ds(r, half_rows, stride=RADIX), :] = (
                    o[h * half_rows:(h + 1) * half_rows, j * LANE:(j + 1) * LANE])
    o_ref[...] = jnp.concatenate(
        [jnp.concatenate([a_ref[j * 2 + h, 0:SEQ // 2, :] for h in range(2)], axis=0) for j in range(halves)],
        axis=1).astype(o_ref.dtype)


def _hyena_branch(xb, w_in_b, conv_w, conv_b, skip, kre, kim, wdft, vdft):
    batch = xb.shape[0]
    nct = HYENA_WIDTH // CT
    wcol = lambda base: pl.BlockSpec((D_MODEL, CT), lambda c, b: (0, base // CT + c))
    ccol = lambda rows, part: pl.BlockSpec((rows, CT), lambda c, b: (0, part * nct + c))
    tile = pl.BlockSpec((SEQ, CT), lambda c, b: (0, c))
    return pl.pallas_call(
        _hyena_kernel,
        grid=(nct, batch),
        in_specs=[
            pl.BlockSpec((None, SEQ, D_MODEL), lambda c, b: (b, 0, 0)),
            wcol(COL_HY), wcol(COL_HY + HYENA_WIDTH), wcol(COL_HY + 2 * HYENA_WIDTH), wcol(COL_HGATE),
            ccol(3, 0), ccol(3, 1), ccol(3, 2), ccol(1, 0), ccol(1, 1), ccol(1, 2),
            pl.BlockSpec((1, CT), lambda c, b: (0, c)),
            tile, tile,
            pl.BlockSpec((RADIX, 2 * SUB, SUB), lambda c, b: (0, 0, 0)),
            pl.BlockSpec((RADIX, SUB, 2 * SUB), lambda c, b: (0, 0, 0)),
        ],
        out_specs=pl.BlockSpec((None, SEQ, CT), lambda c, b: (b, 0, c)),
        out_shape=jax.ShapeDtypeStruct((batch, SEQ, HYENA_WIDTH), bf16),
        scratch_shapes=[pltpu.VMEM((SEQ + 2 * SUB, CT), f32), pltpu.VMEM((SEQ + 2 * SUB, CT), f32),
                        pltpu.VMEM((SEQ + 2 * SUB, CT), f32), pltpu.VMEM((SEQ, CT), f32),
                        pltpu.VMEM((SEQ, CT), f32), pltpu.VMEM((SEQ, CT), bf16),
                        pltpu.VMEM((RADIX * (CT // LANE), 2 * SUB, LANE), f32),
                        pltpu.VMEM((RADIX, 2 * SUB, CT), bf16)],
        compiler_params=pltpu.CompilerParams(dimension_semantics=("arbitrary", "arbitrary"),
                                             vmem_limit_bytes=VMEM_LIMIT),
        name="hyena_branch",
    )(xb, w_in_b, w_in_b, w_in_b, w_in_b, conv_w, conv_w, conv_w, conv_b, conv_b, conv_b,
      skip, kre, kim, wdft, vdft)


ATT_COLS = 2 * ATTN_WIDTH + KV_WIDTH
PROJ_TM = 1024
PROJ_TN = 256
LOG2E = math.log2(math.e)


def _attn_proj_kernel(*refs):
    n_slabs = D_MODEL // LANE
    x_slabs = refs[:n_slabs]
    w_ref, wvt_ref, o_ref, vt_ref, xb_ref = refs[n_slabs:]
    for r in range(RADIX):
        xb_ref[r] = jnp.concatenate(
            [slab[pl.ds(r, PROJ_TM // RADIX, stride=RADIX), :] for slab in x_slabs], axis=1).astype(bf16)
    x = jnp.concatenate([slab[...] for slab in x_slabs], axis=1).astype(bf16)
    q_scale = HEAD_DIM ** -0.5 * LOG2E
    for j in range(ATT_COLS // PROJ_TN):
        cols = slice(j * PROJ_TN, (j + 1) * PROJ_TN)
        acc = jnp.dot(x, w_ref[:, cols], preferred_element_type=f32)
        if (j + 1) * PROJ_TN <= ATTN_WIDTH:
            acc = acc * q_scale
        o_ref[:, cols] = acc.astype(o_ref.dtype)
    vt = lax.dot_general(wvt_ref[...], x, (((1,), (1,)), ((), ())), preferred_element_type=f32)
    for j in range(PROJ_TM // BLOCK):
        vt_ref[j] = vt[:, j * BLOCK:(j + 1) * BLOCK].astype(vt_ref.dtype)


def _attn_proj(x2d, w_att, wvt):
    rows = x2d.shape[0]
    tiles_per_seq = SEQ // PROJ_TM
    n_slabs = D_MODEL // LANE
    return pl.pallas_call(
        _attn_proj_kernel,
        grid=(rows // PROJ_TM,),
        in_specs=[pl.BlockSpec((PROJ_TM, LANE), functools.partial(lambda j, i: (i, j), j))
                  for j in range(n_slabs)]
                 + [pl.BlockSpec((D_MODEL, ATT_COLS), lambda i: (0, 0)),
                    pl.BlockSpec((KV_WIDTH, D_MODEL), lambda i: (0, 0))],
        out_specs=[pl.BlockSpec((PROJ_TM, ATT_COLS), lambda i: (i, 0)),
                   pl.BlockSpec((PROJ_TM // BLOCK, KV_WIDTH, BLOCK), lambda i: (i, 0, 0)),
                   pl.BlockSpec((None, RADIX, PROJ_TM // RADIX, D_MODEL),
                                lambda i: (i // tiles_per_seq, 0, i % tiles_per_seq, 0))],
        out_shape=[jax.ShapeDtypeStruct((rows, ATT_COLS), bf16),
                   jax.ShapeDtypeStruct((rows // BLOCK, KV_WIDTH, BLOCK), bf16),
                   jax.ShapeDtypeStruct((rows // SEQ, RADIX, SUB, D_MODEL), bf16)],
        compiler_params=pltpu.CompilerParams(dimension_semantics=("arbitrary",),
                                             vmem_limit_bytes=VMEM_LIMIT),
        name="attn_proj",
    )(*([x2d] * n_slabs), w_att, wvt)


N_QBLOCKS = SEQ // BLOCK
BAND = 3 * BLOCK
GQ = GROUP * BLOCK


PAIR_ROWS = 2 * BLOCK
PAIRS_PER_STEP = 2
MERGE_TN = 512


def _attn_merge_kernel(relb_ref, sink_ref, bucket_ref, q_ref, g_ref, k_ref, vt_ref, x_ref, yh_ref,
                       wbr_ref, wa_ref, wh_ref, wo_ref, lng_ref, lnb_ref, o_ref,
                       bias_ref, ya_ref, br_ref, ph_ref, pa_ref):
    pair = pl.program_id(1)

    def mm(a, b):
        return jnp.dot(a, b, preferred_element_type=f32)

    @pl.when((pl.program_id(0) == 0) & (pair == 0))
    def _():
        bucket = bucket_ref[...]
        neg = jnp.full((BLOCK, GQ), NEG_INF, f32)
        for h in range(N_KV_HEADS):
            cols = []
            for g in range(GROUP):
                acc = jnp.full(bucket.shape, NEG_INF, f32)
                for kb in range(REL_BUCKETS):
                    acc = jnp.where(bucket == kb, relb_ref[kb, h * GROUP + g] * LOG2E, acc)
                cols.append(acc)
            tbl = jnp.concatenate(cols, axis=1)
            bias_ref[0, h] = tbl
            bias_ref[1, h] = jnp.concatenate([neg, tbl[BLOCK:]], axis=0)
            bias_ref[2, h] = jnp.concatenate([tbl[:2 * BLOCK], neg], axis=0)

    sink_rows = [
        jnp.concatenate([jnp.full((1, BLOCK), sink_ref[h * GROUP + g] * LOG2E, f32) for g in range(GROUP)], axis=1)
        for h in range(N_KV_HEADS)]

    xb = x_ref[...].astype(bf16)
    blocks_per_pair = PAIR_ROWS // BLOCK

    def wide(dep):
        return jnp.concatenate([dep] * (D_MODEL // MERGE_TN), axis=1)

    def gate_proj(a, c):
        def run(dep):
            ra = slice(a * PAIR_ROWS, (a + 1) * PAIR_ROWS)
            cols = slice(c * MERGE_TN, (c + 1) * MERGE_TN)
            res = mm(xb[ra], _anchor(wbr_ref[:, cols], dep))
            br_ref[ra, cols] = 1.0 / (1.0 + jnp.exp(-res))
            return res
        return run

    def hyena_proj(a, c):
        def run(dep):
            ra = slice(a * PAIR_ROWS, (a + 1) * PAIR_ROWS)
            cols = slice(c * MERGE_TN, (c + 1) * MERGE_TN)
            res = mm(yh_ref[ra, :], _anchor(wh_ref[:, cols], dep))
            ph_ref[ra, cols] = br_ref[ra, D_MODEL + c * MERGE_TN:D_MODEL + (c + 1) * MERGE_TN] * res
            return res
        return run

    def attn_proj_pair(a, h0, accumulate):
        def run(dep):
            ra = slice(a * PAIR_ROWS, (a + 1) * PAIR_ROWS)
            lo, hi = h0 * GROUP * HEAD_DIM, (h0 + 2) * GROUP * HEAD_DIM
            res = None
            for c in range(D_MODEL // MERGE_TN):
                cols = slice(c * MERGE_TN, (c + 1) * MERGE_TN)
                part = mm(ya_ref[ra, lo:hi], _anchor(wa_ref[lo:hi, cols], dep))
                pa_ref[ra, cols] = pa_ref[ra, cols] + part if accumulate else part
                res = part
            return res
        return run

    def out_proj(a):
        def run(dep):
            ra = slice(a * PAIR_ROWS, (a + 1) * PAIR_ROWS)
            merged = br_ref[ra, :D_MODEL] * pa_ref[ra, :] + ph_ref[ra, :]
            out = mm(merged.astype(bf16), _anchor(wo_ref[...], wide(dep)))
            pa_ref[ra, :] = out
            return out[:, :MERGE_TN]
        return run

    def layer_norm(a):
        def run(dep):
            del dep
            ra = slice(a * PAIR_ROWS, (a + 1) * PAIR_ROWS)
            r = DEEPNORM_ALPHA * x_ref[ra, :] + pa_ref[ra, :]
            mu = jnp.mean(r, axis=-1, keepdims=True)
            d = r - mu
            var = jnp.mean(d * d, axis=-1, keepdims=True)
            y = d * lax.rsqrt(var + LN_EPS) * lng_ref[...] + lnb_ref[...]
            o_ref[ra, :] = y
            return y[:, :MERGE_TN]
        return run

    items = [(a, h, j) for a in range(PAIRS_PER_STEP) for h in range(N_KV_HEADS) for j in range(blocks_per_pair)]
    per_pair = N_KV_HEADS * blocks_per_pair

    slots = [[] for _ in items]
    for a in range(PAIRS_PER_STEP):
        if a == 0:
            plan = {0: [gate_proj(a, 0)], 1: [gate_proj(a, 1)], 2: [gate_proj(a, 2)], 3: [gate_proj(a, 3)]}
        else:
            plan = {-1: [gate_proj(a, 0)], 0: [attn_proj_pair(a - 1, 2, True), gate_proj(a, 1)],
                    1: [out_proj(a - 1)], 2: [gate_proj(a, 2)], 3: [layer_norm(a - 1), gate_proj(a, 3)]}
        plan.update({4: [hyena_proj(a, 0)], 5: [hyena_proj(a, 1)], 6: [attn_proj_pair(a, 0, False)]})
        for k, ops in plan.items():
            slots[a * per_pair + k].extend(ops)

    def geometry(a, j):
        n = (pair * PAIRS_PER_STEP + a) * blocks_per_pair + j
        pj = jnp.maximum(n - 1, 0)
        nj = jnp.minimum(n + 1, N_QBLOCKS - 1)
        variant = jnp.where(n == 0, 1, jnp.where(n == N_QBLOCKS - 1, 2, 0))
        return n, pj, nj, variant

    def scores(a, h, j):
        n, pj, nj, _ = geometry(a, j)
        hd = slice(h * HEAD_DIM, (h + 1) * HEAD_DIM)
        blk = lambda i: k_ref[pl.ds(pl.multiple_of(i * BLOCK, BLOCK), BLOCK), hd]
        kband = jnp.concatenate([blk(pj), blk(n), blk(nj)], axis=0)
        r0 = a * PAIR_ROWS + j * BLOCK
        qs = jnp.concatenate(
            [q_ref[r0:r0 + BLOCK, (h * GROUP + g) * HEAD_DIM:(h * GROUP + g + 1) * HEAD_DIM] for g in range(GROUP)],
            axis=0)
        return lax.dot_general(kband, qs, (((1,), (1,)), ((), ())), preferred_element_type=f32)

    st_next = scores(*items[0])
    filled = []
    for idx, (a, h, j) in enumerate(items):
        n, pj, nj, variant = geometry(a, j)
        hd = slice(h * HEAD_DIM, (h + 1) * HEAD_DIM)
        rows = slice(a * PAIR_ROWS + j * BLOCK, a * PAIR_ROWS + (j + 1) * BLOCK)
        st = st_next + bias_ref[variant, h]
        ones = jnp.ones((PACK_ROWS, BAND), f32)
        if idx + 1 < len(items):
            st_next = scores(*items[idx + 1])
            ones = ones + _exact_zero(st_next[:PACK_ROWS, :BAND])
        for res in filled:
            ones = ones + _exact_zero(res[:PACK_ROWS, :BAND])
        started = _exact_zero(st[:PACK_ROWS, :])
        filled = [op(started) for op in slots[idx]]
        vtband = jnp.concatenate([vt_ref[pj, hd, :], vt_ref[n, hd, :], vt_ref[nj, hd, :]], axis=1)
        vtband = jnp.concatenate([vtband, ones.astype(bf16)], axis=0)
        m = jnp.maximum(jnp.max(st, axis=0, keepdims=True), sink_rows[h])
        e = jnp.exp2(st - m)
        ot = mm(vtband, e.astype(bf16))
        den = ot[HEAD_DIM:HEAD_DIM + 1] + jnp.exp2(sink_rows[h] - m)
        ot = ot[:HEAD_DIM] * (1.0 / den)
        for p in range(GROUP // 2):
            two = jnp.concatenate([ot[:, (2 * p) * BLOCK:(2 * p + 1) * BLOCK],
                                   ot[:, (2 * p + 1) * BLOCK:(2 * p + 2) * BLOCK]], axis=0)
            c0 = (h * GROUP + 2 * p) * HEAD_DIM
            gate = g_ref[rows, c0:c0 + 2 * HEAD_DIM].astype(f32)
            ya_ref[rows, c0:c0 + 2 * HEAD_DIM] = (two.T * (gate / (1.0 + jnp.exp(-gate)))).astype(bf16)

    no_dep = jnp.zeros((PACK_ROWS, MERGE_TN), f32)
    last = PAIRS_PER_STEP - 1
    attn_proj_pair(last, 2, True)(no_dep)
    out_proj(last)(no_dep)
    layer_norm(last)(no_dep)


def _attn_merge(u_att, vt, x2d, yh, rel_bias, sink, bucket_t, wbr, wa, wh, wo, ln_g, ln_b, batch):
    step_rows = PAIRS_PER_STEP * PAIR_ROWS
    steps = SEQ // step_rows
    row = lambda width, col=0: pl.BlockSpec((step_rows, width), lambda b, p: (b * steps + p, col))
    full = lambda shape: pl.BlockSpec(shape, lambda b, p: (0,) * len(shape))
    return pl.pallas_call(
        _attn_merge_kernel,
        grid=(batch, steps),
        in_specs=[
            pl.BlockSpec(memory_space=pltpu.SMEM),
            pl.BlockSpec(memory_space=pltpu.SMEM),
            full((BAND, BLOCK)),
            row(ATTN_WIDTH, 0), row(ATTN_WIDTH, 1),
            pl.BlockSpec((SEQ, KV_WIDTH), lambda b, p: (b, 2 * ATTN_WIDTH // KV_WIDTH)),
            pl.BlockSpec((N_QBLOCKS, KV_WIDTH, BLOCK), lambda b, p: (b, 0, 0)),
            row(D_MODEL), row(HYENA_WIDTH),
            full((D_MODEL, 2 * D_MODEL)), full((ATTN_WIDTH, D_MODEL)), full((HYENA_WIDTH, D_MODEL)),
            full((D_MODEL, D_MODEL)), full((1, D_MODEL)), full((1, D_MODEL)),
        ],
        out_specs=row(D_MODEL),
        out_shape=jax.ShapeDtypeStruct((batch * SEQ, D_MODEL), f32),
        scratch_shapes=[pltpu.VMEM((3, N_KV_HEADS, BAND, GQ), f32), pltpu.VMEM((step_rows, ATTN_WIDTH), bf16),
                        pltpu.VMEM((step_rows, 2 * D_MODEL), f32), pltpu.VMEM((step_rows, D_MODEL), f32),
                        pltpu.VMEM((step_rows, D_MODEL), f32)],
        compiler_params=pltpu.CompilerParams(dimension_semantics=("arbitrary", "arbitrary"),
                                             vmem_limit_bytes=VMEM_LIMIT),
        name="attn_merge",
    )(rel_bias, sink, bucket_t, u_att, u_att, u_att, vt, x2d, yh, wbr, wa, wh, wo, ln_g, ln_b)


def _layer(x, w_in, rel_bias, attn_sink, conv_w, conv_b, filt_w1, filt_b1, filt_w2, filt_b2,
           filt_w3, filt_b3, filt_w4, filt_freq, hyena_skip, w_branch_attn, w_branch_hyena,
           w_out, ln_g, ln_b):
    batch = x.shape[0]
    wdft_np, vdft_np = _dft_matrices()
    feat_np, decay_np = _filter_constants()
    wdft = jnp.asarray(wdft_np, dtype=bf16)
    vdft = jnp.asarray(vdft_np, dtype=bf16)
    row2d = lambda v: v.reshape(1, -1).astype(f32)

    w1p = jnp.pad(filt_w1.astype(f32), ((0, FEAT_PAD - FILTER_EMB), (0, 0)))
    kre, kim = _filter_spectrum(w1p, row2d(filt_b1), filt_w2.astype(f32), row2d(filt_b2),
                                filt_w3.astype(f32), row2d(filt_b3), row2d(filt_freq),
                                filt_w4.astype(f32), jnp.asarray(feat_np), jnp.asarray(decay_np), wdft)

    w_in_b = w_in.astype(bf16)
    w_att = jnp.concatenate([w_in_b[:, COL_Q:COL_K], w_in_b[:, COL_AGATE:COL_HY],
                             w_in_b[:, COL_K:COL_V]], axis=1)
    wvt = w_in_b[:, COL_V:COL_AGATE].T
    x2d = x.reshape(batch * SEQ, D_MODEL)
    u_att, vt, xb = _attn_proj(x2d, w_att, wvt)
    y_h = _hyena_branch(xb.reshape(batch, SEQ, D_MODEL), w_in_b, conv_w.astype(f32), row2d(conv_b),
                        row2d(hyena_skip), kre, kim, wdft, vdft)
    out = _attn_merge(u_att, vt, x2d, y_h.reshape(batch * SEQ, HYENA_WIDTH), rel_bias.astype(f32),
                      attn_sink.astype(f32), jnp.asarray(_bucket_table()), w_in_b[:, COL_BR:],
                      w_branch_attn.astype(bf16), w_branch_hyena.astype(bf16), w_out.astype(bf16),
                      row2d(ln_g), row2d(ln_b), batch)
    return out.reshape(x.shape).astype(x.dtype)


def kernel(x, w_in, rel_bias, attn_sink, conv_w, conv_b, filt_w1, filt_b1, filt_w2, filt_b2,
           filt_w3, filt_b3, filt_w4, filt_freq, hyena_skip, w_branch_attn, w_branch_hyena,
           w_out, ln_g, ln_b):
    h = x
    for l in range(DEPTH):
        h = _layer(h, w_in[l], rel_bias, attn_sink[l], conv_w[l], conv_b[l],
                   filt_w1[l], filt_b1[l], filt_w2[l], filt_b2[l], filt_w3[l], filt_b3[l],
                   filt_w4[l], filt_freq[l], hyena_skip[l], w_branch_attn[l],
                   w_branch_hyena[l], w_out[l], ln_g[l], ln_b[l])
    return h
```
